```python
import math
import jax, jax.numpy as jnp
from jax import lax
import numpy as np

D_MODEL = 1024
BATCH = 16
SEQ = 4096
DEPTH = 4


DA_HEADS = 4
DA_QK_DIM = 64
DA_V_DIM = 2 * DA_QK_DIM
DA_WIDTH = DA_HEADS * DA_V_DIM
Q_BLOCK = 128
HG_HEADS = 4
HG_K = 128
HG_V = 128
HG_KW = HG_HEADS * HG_K
HG_WIDTH = HG_HEADS * HG_V
HG_CHUNK = 64
SG_GROUPS = 8
SG_CHUNK = 128
SG_WIDTH = D_MODEL
SG_GROUP_DIM = SG_WIDTH // SG_GROUPS
REL_BUCKETS = 32
REL_MAX_DIST = 128
D_FF = -(-8 * D_MODEL // (3 * 256)) * 256
N_EVEN = (DEPTH + 1) // 2
N_ODD = DEPTH // 2
EVEN_SPLITS = (DA_WIDTH, DA_WIDTH, DA_WIDTH, HG_KW, HG_KW, HG_KW, HG_WIDTH, HG_WIDTH)
EVEN_IN = sum(EVEN_SPLITS)
ODD_IN = 2 * SG_WIDTH
EPS = 1e-6

kernel_name = 'hybrid_diffattn_hgrn2_sgu_encoder'


def rms_norm(x, gain):
    xf = x.astype(jnp.float32)
    y = xf * lax.rsqrt(jnp.mean(xf * xf, axis=-1, keepdims=True) + EPS)
    return (y * gain.astype(jnp.float32)).astype(x.dtype)


def rel_bucket(rel):
    half = REL_BUCKETS // 2
    max_exact = half // 2
    ret = jnp.where(rel > 0, half, 0)
    n = jnp.abs(rel)
    nf = jnp.maximum(n, 1).astype(jnp.float32)
    large = max_exact + (jnp.log(nf / max_exact) / math.log(REL_MAX_DIST / max_exact)
                         * (half - max_exact)).astype(jnp.int32)
    large = jnp.minimum(large, half - 1)
    return ret + jnp.where(n < max_exact, n, large)


def diff_attention(q, k, v, lam, rel_bias):
    b, s, h, _ = q.shape
    n_blk = s // Q_BLOCK
    q = q * (DA_QK_DIM ** -0.5)
    q_blocks = q.reshape(b, n_blk, Q_BLOCK, h, 2 * DA_QK_DIM).transpose(1, 0, 3, 2, 4)
    k = k.transpose(0, 2, 1, 3)
    k1, k2 = k[..., :DA_QK_DIM], k[..., DA_QK_DIM:]
    v = v.transpose(0, 2, 1, 3)
    k_pos = jnp.arange(s, dtype=jnp.int32)

    def one_block(args):
        qb, blk = args
        q_pos = blk * Q_BLOCK + jnp.arange(Q_BLOCK, dtype=jnp.int32)
        bias = rel_bias[rel_bucket(k_pos[None, :] - q_pos[:, None])]
        bias = bias.astype(jnp.float32).transpose(2, 0, 1)[None]
        s1 = jnp.einsum('bhqd,bhkd->bhqk', qb[..., :DA_QK_DIM], k1).astype(jnp.float32) + bias
        s2 = jnp.einsum('bhqd,bhkd->bhqk', qb[..., DA_QK_DIM:], k2).astype(jnp.float32) + bias
        p = jax.nn.softmax(s1, axis=-1) - lam * jax.nn.softmax(s2, axis=-1)
        return jnp.einsum('bhqk,bhkd->bhqd', p.astype(v.dtype), v)

    o = lax.map(one_block, (q_blocks, jnp.arange(n_blk, dtype=jnp.int32)))
    return o.transpose(1, 0, 3, 2, 4).reshape(b, s, h, DA_V_DIM)


def forget_gate(z, lb):
    z = z.astype(jnp.float32)
    lb = lb.reshape(HG_HEADS, HG_K)
    log_f = jnp.logaddexp(jnp.log(lb), jnp.log1p(-lb) + jax.nn.log_sigmoid(z))
    k = (1.0 - lb) * jax.nn.sigmoid(-z)
    return log_f, k


def chunk_gla(q, k, v, log_f):
    b, s, h, dk = q.shape
    dv = v.shape[-1]
    n = s // HG_CHUNK

    def to_chunks(t):
        return t.reshape(b, n, HG_CHUNK, h, t.shape[-1]).transpose(1, 0, 3, 2, 4)

    qc, kc, vc, lc = to_chunks(q), to_chunks(k), to_chunks(v), to_chunks(log_f)
    mask = jnp.tril(jnp.ones((HG_CHUNK, HG_CHUNK), dtype=bool))[:, :, None]

    def step(state, inp):
        q_, k_, v_, l_ = inp
        cum = jnp.cumsum(l_, axis=-2)
        o_inter = jnp.einsum('bhck,bhkv->bhcv', q_ * jnp.exp(cum), state)
        diff = cum[..., :, None, :] - cum[..., None, :, :]
        decay = jnp.exp(jnp.where(mask, diff, -jnp.inf))
        att = jnp.einsum('bhik,bhjk,bhijk->bhij', q_, k_, decay)
        o = o_inter + jnp.einsum('bhij,bhjv->bhiv', att, v_)
        last = cum[..., -1:, :]
        state = (jnp.exp(last[..., 0, :])[..., None] * state
                 + jnp.einsum('bhck,bhcv->bhkv', k_ * jnp.exp(last - cum), v_))
        return state, o

    s0 = jnp.zeros((b, h, dk, dv), jnp.float32)
    _, o = lax.scan(step, s0, (qc, kc, vc, lc))
    return o.transpose(1, 0, 3, 2, 4).reshape(b, s, h, dv)


def hgrn2_bidir(q, z_fwd, z_bwd, i, lb_fwd, lb_bwd):
    lf_f, k_f = forget_gate(z_fwd, lb_fwd)
    lf_b, k_b = forget_gate(z_bwd, lb_bwd)
    flip = lambda t: jnp.flip(t, axis=1)
    qf = q.astype(jnp.float32)
    vf = i.astype(jnp.float32)
    q2 = jnp.concatenate([qf, flip(qf)], axis=2)
    k2 = jnp.concatenate([k_f, flip(k_b)], axis=2)
    l2 = jnp.concatenate([lf_f, flip(lf_b)], axis=2)
    v2 = jnp.concatenate([vf, flip(vf)], axis=2)
    o2 = chunk_gla(q2, k2, v2, l2)
    return o2[:, :, :HG_HEADS] + flip(o2[:, :, HG_HEADS:])


def even_mixer(h, w_in, w_out, lam, lambda_init, subln, rel_bias, lb_fwd, lb_bwd, hg_norm):
    b, s, _ = h.shape
    da_q, da_k, da_v, hg_q, hg_zf, hg_zb, hg_i, hg_g = jnp.split(
        h @ w_in, np.cumsum(EVEN_SPLITS)[:-1].tolist(), axis=-1)
    o_a = diff_attention(da_q.reshape(b, s, DA_HEADS, 2 * DA_QK_DIM),
                         da_k.reshape(b, s, DA_HEADS, 2 * DA_QK_DIM),
                         da_v.reshape(b, s, DA_HEADS, DA_V_DIM), lam, rel_bias)
    o_a = (rms_norm(o_a, subln) * (1.0 - lambda_init)).reshape(b, s, DA_WIDTH)
    heads = lambda t, d: t.reshape(b, s, HG_HEADS, d)
    o_b = hgrn2_bidir(jax.nn.silu(heads(hg_q, HG_K)), heads(hg_zf, HG_K), heads(hg_zb, HG_K),
                      heads(hg_i, HG_V), lb_fwd, lb_bwd).astype(h.dtype)
    o_b = (rms_norm(o_b, hg_norm) * jax.nn.silu(heads(hg_g, HG_V))).reshape(b, s, HG_WIDTH)
    return jnp.concatenate([o_a, o_b], axis=-1) @ w_out


def odd_mixer(h, w_in, sg_norm, sg_w, sg_b, w_out):
    b, s, _ = h.shape
    u, v = jnp.split(jax.nn.gelu(h @ w_in, approximate=False), 2, axis=-1)
    v = rms_norm(v, sg_norm)
    n = s // SG_CHUNK
    v = v.reshape(b, n, SG_CHUNK, SG_GROUPS, SG_GROUP_DIM)
    v = jnp.einsum('gpq,bnqgc->bnpgc', sg_w, v) + sg_b.T[None, None, :, :, None]
    return (u * v.reshape(b, s, SG_WIDTH)) @ w_out


def swiglu(h, w_in, w_out):
    gate, up = jnp.split(h @ w_in, 2, axis=-1)
    return (jax.nn.silu(gate) * up) @ w_out


def setup_inputs(seed: int = 0) -> dict:
    key = jax.random.key(seed)
    ks = jax.random.split(key, 22)

    def nrm(k, shape, scale):
        return jax.random.normal(k, shape, jnp.float32) * scale

    return {
        'x': nrm(ks[0], (BATCH, SEQ, D_MODEL), 1.0),
        'rel_bias': nrm(ks[1], (REL_BUCKETS, DA_HEADS), 0.5),
        'norm_mix': 1.0 + nrm(ks[2], (DEPTH, D_MODEL), 0.02),
        'norm_ffn': 1.0 + nrm(ks[3], (DEPTH, D_MODEL), 0.02),
        'norm_final': 1.0 + nrm(ks[4], (D_MODEL,), 0.02),
        'w_in_even': nrm(ks[5], (N_EVEN, D_MODEL, EVEN_IN), D_MODEL ** -0.5),
        'w_out_even': nrm(ks[6], (N_EVEN, DA_WIDTH + HG_WIDTH, D_MODEL), (DA_WIDTH + HG_WIDTH) ** -0.5),
        'lambda_q1': nrm(ks[7], (N_EVEN, DA_QK_DIM), 0.1),
        'lambda_k1': nrm(ks[8], (N_EVEN, DA_QK_DIM), 0.1),
        'lambda_q2': nrm(ks[9], (N_EVEN, DA_QK_DIM), 0.1),
        'lambda_k2': nrm(ks[10], (N_EVEN, DA_QK_DIM), 0.1),
        'da_subln': 1.0 + nrm(ks[11], (N_EVEN, DA_V_DIM), 0.02),
        'hg_lb_fwd': nrm(ks[12], (N_EVEN, HG_KW), 0.1),
        'hg_lb_bwd': nrm(ks[13], (N_EVEN, HG_KW), 0.1),
        'hg_norm': 1.0 + nrm(ks[14], (N_EVEN, HG_V), 0.02),
        'w_in_odd': nrm(ks[15], (N_ODD, D_MODEL, ODD_IN), D_MODEL ** -0.5),
        'sg_norm': 1.0 + nrm(ks[16], (N_ODD, SG_WIDTH), 0.02),
        'sg_w': nrm(ks[17], (N_ODD, SG_GROUPS, SG_CHUNK, SG_CHUNK), SG_CHUNK ** -0.5),
        'sg_b': 1.0 + nrm(ks[18], (N_ODD, SG_GROUPS, SG_CHUNK), 0.1),
        'w_out_odd': nrm(ks[19], (N_ODD, SG_WIDTH, D_MODEL), SG_WIDTH ** -0.5),
        'w_ffn_in': nrm(ks[20], (DEPTH, D_MODEL, 2 * D_FF), D_MODEL ** -0.5),
        'w_ffn_out': nrm(ks[21], (DEPTH, D_FF, D_MODEL), D_FF ** -0.5),
    }


def reference(x, rel_bias, norm_mix, norm_ffn, norm_final, w_in_even, w_out_even,
              lambda_q1, lambda_k1, lambda_q2, lambda_k2, da_subln, hg_lb_fwd, hg_lb_bwd, hg_norm,
              w_in_odd, sg_norm, sg_w, sg_b, w_out_odd, w_ffn_in, w_ffn_out):
    f32 = jnp.float32
    lb_f = jnp.cumsum(jax.nn.softmax(hg_lb_fwd.astype(f32), axis=0), axis=0)
    lb_f = lb_f - lb_f[:1]
    lb_b = jnp.cumsum(jax.nn.softmax(hg_lb_bwd.astype(f32), axis=0), axis=0)
    lb_b = lb_b - lb_b[:1]
    h = x
    for l in range(DEPTH):
        hn = rms_norm(h, norm_mix[l])
        if l % 2 == 0:
            e = l // 2
            lambda_init = 0.8 - 0.6 * math.exp(-0.3 * l)
            lam = (jnp.exp(jnp.sum(lambda_q1[e].astype(f32) * lambda_k1[e].astype(f32)))
                   - jnp.exp(jnp.sum(lambda_q2[e].astype(f32) * lambda_k2[e].astype(f32))) + lambda_init)
            h = h + even_mixer(hn, w_in_even[e], w_out_even[e], lam, lambda_init, da_subln[e], rel_bias,
                               lb_f[e], lb_b[e], hg_norm[e])
        else:
            o = l // 2
            h = h + odd_mixer(hn, w_in_odd[o], sg_norm[o], sg_w[o], sg_b[o], w_out_odd[o])
        h = h + swiglu(rms_norm(h, norm_ffn[l]), w_ffn_in[l], w_ffn_out[l])
    return rms_norm(h, norm_final)
```

```python
import functools
import math

import numpy as np
import jax
import jax.numpy as jnp
from jax import lax
from jax.experimental import pallas as pl
from jax.experimental.pallas import tpu as pltpu

F32 = jnp.float32
BF16 = jnp.bfloat16

D_MODEL = 1024
DEPTH = 4
DA_HEADS = 4
DA_QK_DIM = 64
DA_V_DIM = 128
DA_WIDTH = DA_HEADS * DA_V_DIM
HG_HEADS = 4
HG_K = 128
HG_V = 128
HG_WIDTH = HG_HEADS * HG_V
SG_GROUPS = 8
SG_CHUNK = 128
SG_GROUP_DIM = D_MODEL // SG_GROUPS
REL_BUCKETS = 32
REL_MAX_DIST = 128
D_FF = 2816
EPS = 1e-6

LANES = 128
VMEM_LIMIT = 56 * 1024 * 1024

ROW_TILE = 512
ATTN_TQ = 256
ATTN_TK = 512
HG_BLOCK = 512
HG_CHUNK = 64
HG_LEVELS = 6
FFN_CHUNK = 1408


def _cparams(sem):
    return pltpu.CompilerParams(dimension_semantics=sem, vmem_limit_bytes=VMEM_LIMIT)


def _resident(shape):
    nd = len(shape)
    return pl.BlockSpec(shape, lambda *_: (0,) * nd, pipeline_mode=pl.Buffered(1))


def _rms(x, gain):
    ms = jnp.mean(x * x, axis=-1, keepdims=True)
    return x * lax.rsqrt(ms + EPS) * gain


def _silu(x):
    return x / (1.0 + jnp.exp(-x))


def _dot(a, b):
    return jnp.dot(a, b, preferred_element_type=F32)


def _dot_nt(a, b):
    return lax.dot_general(a, b, (((1,), (1,)), ((), ())), preferred_element_type=F32)


def _dot_tn(a, b):
    return lax.dot_general(a, b, (((0,), (0,)), ((), ())), preferred_element_type=F32)


def _band_kernel(rb_ref, o_ref):
    h = pl.program_id(0)
    shape = (LANES, 3 * LANES)
    qi = lax.broadcasted_iota(jnp.int32, shape, 0)
    xi = lax.broadcasted_iota(jnp.int32, shape, 1)
    rel = xi - LANES - qi
    half = REL_BUCKETS // 2
    max_exact = half // 2
    ret = jnp.where(rel > 0, half, 0)
    n = jnp.abs(rel)
    nf = jnp.maximum(n, 1).astype(F32)
    large = max_exact + (jnp.log(nf / max_exact) / math.log(REL_MAX_DIST / max_exact)
                         * (half - max_exact)).astype(jnp.int32)
    large = jnp.minimum(large, half - 1)
    bucket = ret + jnp.where(n < max_exact, n, large)
    out = jnp.zeros(shape, F32)
    for bkt in range(REL_BUCKETS):
        out = jnp.where(bucket == bkt, rb_ref[bkt * DA_HEADS + h], out)
    o_ref[...] = out


def _band_call(rel_bias):
    return pl.pallas_call(
        _band_kernel,
        grid=(DA_HEADS,),
        in_specs=[pl.BlockSpec(memory_space=pltpu.SMEM)],
        out_specs=pl.BlockSpec((None, LANES, 3 * LANES), lambda h: (h, 0, 0)),
        out_shape=jax.ShapeDtypeStruct((DA_HEADS, LANES, 3 * LANES), F32),
        name="rel_bias_band",
    )(rel_bias.reshape(-1))


def _even_in_kernel(x_ref, g_ref, wa_ref, wb_ref, oa_ref, ob_ref):
    hn = _rms(x_ref[...], g_ref[...]).astype(BF16)
    oa_ref[...] = _dot(hn, wa_ref[...]).astype(BF16)
    ob_ref[...] = _dot(hn, wb_ref[...])


def _even_in_call(h, gain, w_attn, w_hg):
    t = h.shape[0]
    tm = min(ROW_TILE, t)
    na, nb = w_attn.shape[1], w_hg.shape[1]
    return pl.pallas_call(
        _even_in_kernel,
        grid=(t // tm,),
        in_specs=[pl.BlockSpec((tm, D_MODEL), lambda i: (i, 0)),
                  _resident((1, D_MODEL)),
                  _resident((D_MODEL, na)),
                  _resident((D_MODEL, nb))],
        out_specs=[pl.BlockSpec((tm, na), lambda i: (i, 0)),
                   pl.BlockSpec((tm, nb), lambda i: (i, 0))],
        out_shape=[jax.ShapeDtypeStruct((t, na), BF16),
                   jax.ShapeDtypeStruct((t, nb), F32)],
        compiler_params=_cparams(("parallel",)),
        name="even_in_proj",
    )(h, gain, w_attn, w_hg)


def _attn_kernel(rb_ref, lamv_ref, q_ref, k_ref, v_ref, band_ref, sub_ref, o_ref, bias_scr,
                 *, tq, tk, seq, lambda_init):
    h = pl.program_id(0)
    qi = pl.program_id(1)
    b = pl.program_id(2)
    nrb = tq // LANES
    ncb = seq // LANES
    cpb = tk // LANES
    nkb = seq // tk

    @pl.when(b == 0)
    def _():
        far_l = rb_ref[(REL_BUCKETS // 2 - 1) * DA_HEADS + h]
        far_r = rb_ref[(REL_BUCKETS - 1) * DA_HEADS + h]
        for r in range(nrb):
            g = qi * nrb + r
            for c in range(ncb):
                d = jnp.zeros((LANES, LANES), jnp.int32) + (c - g)
                far = jnp.where(d < 0, far_l, far_r)
                tile = jnp.where(d == 0, band_ref[:, LANES:2 * LANES],
                                 jnp.where(d == -1, band_ref[:, 0:LANES],
                                           jnp.where(d == 1, band_ref[:, 2 * LANES:3 * LANES], far)))
                cc = c % cpb
                bias_scr[c // cpb, r * LANES:(r + 1) * LANES, cc * LANES:(cc + 1) * LANES] = tile

    q = q_ref[...] * (DA_QK_DIM ** -0.5)
    lane = lax.broadcasted_iota(jnp.int32, q.shape, 1)
    zero = jnp.zeros_like(q)
    qq = jnp.concatenate([jnp.where(lane < DA_QK_DIM, q, zero),
                          jnp.where(lane < DA_QK_DIM, zero, q)], axis=0)

    def body(kb, carry):
        m, l, acc = carry
        k0 = pl.multiple_of(kb * tk, tk)
        s = _dot_nt(qq, k_ref[pl.ds(k0, tk), :])
        bt = bias_scr[kb]
        s = s + jnp.concatenate([bt, bt], axis=0)
        m_new = jnp.maximum(m, jnp.max(s, axis=-1, keepdims=True))
        alpha = jnp.exp(m - m_new)
        p = jnp.exp(s - m_new)
        l = alpha * l + jnp.sum(p, axis=-1, keepdims=True)
        acc = alpha * acc + _dot(p.astype(BF16), v_ref[pl.ds(k0, tk), :])
        return m_new, l, acc

    m0 = jnp.full((2 * tq, 1), -jnp.inf, F32)
    l0 = jnp.zeros((2 * tq, 1), F32)
    acc0 = jnp.zeros((2 * tq, DA_V_DIM), F32)
    _, l, acc = lax.fori_loop(0, nkb, body, (m0, l0, acc0))

    lamv = lamv_ref[...]
    lam = (jnp.exp(jnp.sum(lamv[0:1] * lamv[1:2], axis=-1, keepdims=True))
           - jnp.exp(jnp.sum(lamv[2:3] * lamv[3:4], axis=-1, keepdims=True)) + lambda_init)
    o = acc[:tq] / l[:tq] - lam * (acc[tq:] / l[tq:])
    o_ref[...] = (_rms(o, sub_ref[...]) * (1.0 - lambda_init)).astype(BF16)


def _attn_call(qkv, band, rel_bias, lamv, subln, lambda_init):
    b, s, _ = qkv.shape
    tq = min(ATTN_TQ, s)
    tk = min(ATTN_TK, s)
    kern = functools.partial(_attn_kernel, tq=tq, tk=tk, seq=s, lambda_init=lambda_init)
    return pl.pallas_call(
        kern,
        grid=(DA_HEADS, s // tq, b),
        in_specs=[pl.BlockSpec(memory_space=pltpu.SMEM),
                  _resident((4, DA_QK_DIM)),
                  pl.BlockSpec((None, tq, DA_V_DIM), lambda h, i, bb: (bb, i, h)),
                  pl.BlockSpec((None, s, DA_V_DIM), lambda h, i, bb: (bb, 0, DA_HEADS + h)),
                  pl.BlockSpec((None, s, DA_V_DIM), lambda h, i, bb: (bb, 0, 2 * DA_HEADS + h)),
                  pl.BlockSpec((None, LANES, 3 * LANES), lambda h, i, bb: (h, 0, 0)),
                  _resident((1, DA_V_DIM))],
        out_specs=pl.BlockSpec((None, tq, DA_V_DIM), lambda h, i, bb: (bb, i, h)),
        out_shape=jax.ShapeDtypeStruct((b, s, DA_WIDTH), BF16),
        scratch_shapes=[pltpu.VMEM((s // tk, tq, tk), F32)],
        compiler_params=_cparams(("parallel", "parallel", "arbitrary")),
        name="diff_attention",
    )(rel_bias.reshape(-1), lamv, qkv, qkv, qkv, band, subln)


def _hg_sum_matrix(reverse):
    c = HG_CHUNK
    r = np.arange(c)
    blocks = []
    if not reverse:
        blocks.append(r[None, :] <= r[:, None])
        blocks.append(r[None, :] > r[:, None])
    else:
        blocks.append(r[None, :] >= r[:, None])
        blocks.append(r[None, :] < r[:, None])
    for lvl in range(HG_LEVELS):
        hs = 1 << lvl
        base = r & ~(2 * hs - 1)
        upper = ((r >> lvl) & 1) == 1
        m = np.zeros((c, c), bool)
        for i in range(c):
            if not reverse:
                bnd = base[i] + hs - 1
                lo, hi = (bnd + 1, i) if upper[i] else (i + 1, bnd)
            else:
                bnd = base[i] + hs
                lo, hi = (bnd, i - 1) if upper[i] else (i, bnd - 1)
            m[i, lo:hi + 1] = True
        blocks.append(m)
    blocks.append(np.ones((8, c), bool))
    m = np.concatenate(blocks, axis=0).astype(np.float32)
    return np.concatenate([m, m, m], axis=1)


def _hg_chunk(q_ref, z_ref, v_ref, lb_ref, m_ref, o_ref, st_ref, r0, reverse):
    c = HG_CHUNK
    rows = pl.ds(r0, c)
    q = _silu(q_ref[rows, :])
    z = z_ref[rows, :]
    v = v_ref[rows, :].astype(BF16)
    lb = lb_ref[...]

    e = jnp.exp(-jnp.abs(z))
    log_sig = jnp.minimum(z, 0.0) - jnp.log1p(e)
    t = jnp.log1p(-lb) + log_sig
    a = jnp.log(lb)
    log_f = jnp.maximum(a, t) + jnp.log1p(jnp.exp(-jnp.abs(a - t)))
    kk = (1.0 - lb) * (jnp.where(z >= 0, e, 1.0) / (1.0 + e))

    l1 = log_f.astype(BF16)
    r1 = log_f - l1.astype(F32)
    l2 = r1.astype(BF16)
    l3 = (r1 - l2.astype(F32)).astype(BF16)
    sums = _dot(m_ref[...], jnp.concatenate([l1, l2, l3], axis=0))
    cum = sums[0:c]
    rest = sums[c:2 * c]
    tot = sums[(2 + HG_LEVELS) * c:(2 + HG_LEVELS) * c + 1]

    st = st_ref[...]
    o = _dot_nt((q * jnp.exp(cum)).astype(BF16), st.astype(BF16))

    row = lax.broadcasted_iota(jnp.int32, (c, HG_K), 0)
    ri = lax.broadcasted_iota(jnp.int32, (c, c), 0)
    ci = lax.broadcasted_iota(jnp.int32, (c, c), 1)
    att = jnp.where(ri == ci, _dot_nt(q.astype(BF16), kk.astype(BF16)), 0.0)
    for lvl in range(HG_LEVELS):
        dec = jnp.exp(sums[(2 + lvl) * c:(3 + lvl) * c])
        later = ((row >> lvl) & 1) == 1
        q_rows = jnp.logical_not(later) if reverse else later
        x = jnp.where(q_rows, q, kk) * dec
        qt = jnp.where(q_rows, x, 0.0).astype(BF16)
        kt = jnp.where(q_rows, 0.0, x).astype(BF16)
        same = (ri >> (lvl + 1)) == (ci >> (lvl + 1))
        att = att + jnp.where(same, _dot_nt(qt, kt), 0.0)
    o = o + _dot(att.astype(BF16), v)
    o_ref[rows, :] = o

    kd = (kk * jnp.exp(rest)).astype(BF16)
    st_ref[...] = st * jnp.exp(tot) + _dot_tn(v, kd)


def _hgrn_kernel(qf_ref, zf_ref, vf_ref, qb_ref, zb_ref, vb_ref, lbf_ref, lbb_ref, mf_ref, mb_ref,
                 of_ref, ob_ref, sf_ref, sb_ref, *, tb):
    @pl.when(pl.program_id(2) == 0)
    def _():
        sf_ref[...] = jnp.zeros_like(sf_ref)
        sb_ref[...] = jnp.zeros_like(sb_ref)

    nsc = tb // HG_CHUNK

    def body(sc, carry):
        rf = pl.multiple_of(sc * HG_CHUNK, HG_CHUNK)
        rb = pl.multiple_of((nsc - 1 - sc) * HG_CHUNK, HG_CHUNK)
        _hg_chunk(qf_ref, zf_ref, vf_ref, lbf_ref, mf_ref, of_ref, sf_ref, rf, False)
        _hg_chunk(qb_ref, zb_ref, vb_ref, lbb_ref, mb_ref, ob_ref, sb_ref, rb, True)
        return carry

    lax.fori_loop(0, nsc, body, 0)


def _hgrn_call(hg, lb_f, lb_b):
    b, s, _ = hg.shape
    tb = min(HG_BLOCK, s)
    nb = s // tb
    hh = HG_HEADS
    mf = jnp.asarray(_hg_sum_matrix(False), BF16)
    mb = jnp.asarray(_hg_sum_matrix(True), BF16)

    def fwd(col):
        return pl.BlockSpec((None, tb, HG_K), lambda bb, h, i: (bb, i, col * hh + h))

    def bwd(col):
        return pl.BlockSpec((None, tb, HG_K), lambda bb, h, i: (bb, nb - 1 - i, col * hh + h))

    lb_spec = pl.BlockSpec((None, 1, HG_K), lambda bb, h, i: (h, 0, 0))
    return pl.pallas_call(
        functools.partial(_hgrn_kernel, tb=tb),
        grid=(b, hh, nb),
        in_specs=[fwd(0), fwd(1), fwd(3), bwd(0), bwd(2), bwd(3), lb_spec, lb_spec,
                  _resident(mf.shape), _resident(mb.shape)],
        out_specs=[pl.BlockSpec((None, tb, HG_V), lambda bb, h, i: (bb, i, h)),
                   pl.BlockSpec((None, tb, HG_V), lambda bb, h, i: (bb, nb - 1 - i, h))],
        out_shape=[jax.ShapeDtypeStruct((b, s, HG_WIDTH), F32),
                   jax.ShapeDtypeStruct((b, s, HG_WIDTH), F32)],
        scratch_shapes=[pltpu.VMEM((HG_V, HG_K), F32), pltpu.VMEM((HG_V, HG_K), F32)],
        compiler_params=_cparams(("parallel", "parallel", "arbitrary")),
        name="hgrn2_bidir",
    )(hg, hg, hg, hg, hg, hg, lb_f.reshape(hh, 1, HG_K), lb_b.reshape(hh, 1, HG_K), mf, mb)


def _even_out_kernel(oa_ref, of_ref, ob_ref, g_ref, h_ref, gn_ref, w_ref, o_ref):
    ob = of_ref[...] + ob_ref[...]
    parts = [oa_ref[...]]
    for hh in range(HG_HEADS):
        cols = slice(hh * HG_V, (hh + 1) * HG_V)
        y = _rms(ob[:, cols], gn_ref[...]) * _silu(g_ref[:, cols])
        parts.append(y.astype(BF16))
    cat = jnp.concatenate(parts, axis=-1)
    o_ref[...] = h_ref[...] + _dot(cat, w_ref[...])


def _even_out_call(o_a, o_f, o_b, hg, h, hg_norm, w_out):
    t = h.shape[0]
    tm = min(ROW_TILE, t)
    gate_blk = hg.shape[1] // HG_WIDTH - 1
    return pl.pallas_call(
        _even_out_kernel,
        grid=(t // tm,),
        in_specs=[pl.BlockSpec((tm, DA_WIDTH), lambda i: (i, 0)),
                  pl.BlockSpec((tm, HG_WIDTH), lambda i: (i, 0)),
                  pl.BlockSpec((tm, HG_WIDTH), lambda i: (i, 0)),
                  pl.BlockSpec((tm, HG_WIDTH), lambda i: (i, gate_blk)),
                  pl.BlockSpec((tm, D_MODEL), lambda i: (i, 0)),
                  _resident((1, HG_V)),
                  _resident(w_out.shape)],
        out_specs=pl.BlockSpec((tm, D_MODEL), lambda i: (i, 0)),
        out_shape=jax.ShapeDtypeStruct((t, D_MODEL), F32),
        compiler_params=_cparams(("parallel",)),
        name="even_out_proj",
    )(o_a, o_f, o_b, hg, h, hg_norm, w_out)


def _odd_kernel(h_ref, g_ref, win_ref, sgn_ref, sgw_ref, sgb_ref, wout_ref, o_ref, vm_ref, *, tm):
    x = h_ref[...]
    hn = _rms(x, g_ref[...]).astype(BF16)
    y = _dot(hn, win_ref[...])
    y = 0.5 * y * (1.0 + lax.erf(y * math.sqrt(0.5)))
    u = y[:, :D_MODEL]
    v = _rms(y[:, D_MODEL:], sgn_ref[...]).astype(BF16)
    for c in range(tm // SG_CHUNK):
        rows = slice(c * SG_CHUNK, (c + 1) * SG_CHUNK)
        for g in range(SG_GROUPS):
            cols = slice(g * SG_GROUP_DIM, (g + 1) * SG_GROUP_DIM)
            vm_ref[rows, cols] = _dot(sgw_ref[g], v[rows, cols]) + sgb_ref[:, cols]
    t = (u * vm_ref[...]).astype(BF16)
    o_ref[...] = x + _dot(t, wout_ref[...])


def _odd_call(h, gain, w_in, sg_norm, sg_w, sg_b_full, w_out):
    t = h.shape[0]
    tm = min(ROW_TILE, t)
    return pl.pallas_call(
        functools.partial(_odd_kernel, tm=tm),
        grid=(t // tm,),
        in_specs=[pl.BlockSpec((tm, D_MODEL), lambda i: (i, 0)),
                  _resident((1, D_MODEL)),
                  _resident(w_in.shape),
                  _resident((1, D_MODEL)),
                  _resident(sg_w.shape),
                  _resident(sg_b_full.shape),
                  _resident(w_out.shape)],
        out_specs=pl.BlockSpec((tm, D_MODEL), lambda i: (i, 0)),
        out_shape=jax.ShapeDtypeStruct((t, D_MODEL), F32),
        scratch_shapes=[pltpu.VMEM((tm, D_MODEL), F32)],
        compiler_params=_cparams(("parallel",)),
        name="odd_mixer",
    )(h, gain, w_in, sg_norm, sg_w, sg_b_full, w_out)


def _ffn_kernel(h_ref, g_ref, win_ref, wout_ref, fg_ref, o_ref, *, final):
    x = h_ref[...]
    hn = _rms(x, g_ref[...]).astype(BF16)
    acc = x
    for c in range(D_FF // FFN_CHUNK):
        lo = c * FFN_CHUNK
        gate = _dot(hn, win_ref[:, lo:lo + FFN_CHUNK])
        up = _dot(hn, win_ref[:, D_FF + lo:D_FF + lo + FFN_CHUNK])
        act = (_silu(gate) * up).astype(BF16)
        acc = acc + _dot(act, wout_ref[lo:lo + FFN_CHUNK, :])
    if final:
        acc = _rms(acc, fg_ref[...])
    o_ref[...] = acc


def _ffn_call(h, gain, w_in, w_out, final_gain, final):
    t = h.shape[0]
    tm = min(ROW_TILE, t)
    return pl.pallas_call(
        functools.partial(_ffn_kernel, final=final),
        grid=(t // tm,),
        in_specs=[pl.BlockSpec((tm, D_MODEL), lambda i: (i, 0)),
                  _resident((1, D_MODEL)),
                  _resident(w_in.shape),
                  _resident(w_out.shape),
                  _resident((1, D_MODEL))],
        out_specs=pl.BlockSpec((tm, D_MODEL), lambda i: (i, 0)),
        out_shape=jax.ShapeDtypeStruct((t, D_MODEL), F32),
        compiler_params=_cparams(("parallel",)),
        name="swiglu_ffn",
    )(h, gain, w_in, w_out, final_gain)


def _lower_bounds(p):
    lb = jnp.cumsum(jax.nn.softmax(p.astype(F32), axis=0), axis=0)
    return lb - lb[:1]


def kernel(x, rel_bias, norm_mix, norm_ffn, norm_final, w_in_even, w_out_even,
           lambda_q1, lambda_k1, lambda_q2, lambda_k2, da_subln, hg_lb_fwd, hg_lb_bwd, hg_norm,
           w_in_odd, sg_norm, sg_w, sg_b, w_out_odd, w_ffn_in, w_ffn_out):
    b, s, d = x.shape
    t = b * s
    lb_f = _lower_bounds(hg_lb_fwd)
    lb_b = _lower_bounds(hg_lb_bwd)
    band = _band_call(rel_bias.astype(F32))
    n_attn = 3 * DA_WIDTH
    h = x.reshape(t, d)
    for l in range(DEPTH):
        gain = norm_mix[l].reshape(1, d)
        if l % 2 == 0:
            e = l // 2
            lambda_init = 0.8 - 0.6 * math.exp(-0.3 * l)
            w_in = w_in_even[e].astype(BF16)
            qkv, hg = _even_in_call(h, gain, w_in[:, :n_attn], w_in[:, n_attn:])
            lamv = jnp.stack([lambda_q1[e], lambda_k1[e], lambda_q2[e], lambda_k2[e]]).astype(F32)
            o_a = _attn_call(qkv.reshape(b, s, n_attn), band, rel_bias.astype(F32), lamv,
                             da_subln[e].reshape(1, DA_V_DIM), lambda_init)
            o_f, o_b = _hgrn_call(hg.reshape(b, s, -1), lb_f[e], lb_b[e])
            h = _even_out_call(o_a.reshape(t, DA_WIDTH), o_f.reshape(t, HG_WIDTH),
                               o_b.reshape(t, HG_WIDTH), hg, h,
                               hg_norm[e].reshape(1, HG_V), w_out_even[e].astype(BF16))
        else:
            o = l // 2
            sg_b_full = jnp.repeat(sg_b[o].T.astype(F32), SG_GROUP_DIM, axis=1)
            h = _odd_call(h, gain, w_in_odd[o].astype(BF16), sg_norm[o].reshape(1, d),
                          sg_w[o].astype(BF16), sg_b_full, w_out_odd[o].astype(BF16))
        h = _ffn_call(h, norm_ffn[l].reshape(1, d), w_ffn_in[l].astype(BF16),
                      w_ffn_out[l].astype(BF16), norm_final.reshape(1, d), l == DEPTH - 1)
    return h.reshape(b, s, d)
```

```python
import functools
import math

import numpy as np
import jax
import jax.numpy as jnp
from jax import lax
from jax.experimental import pallas as pl
from jax.experimental.pallas import tpu as pltpu

F32 = jnp.float32
BF16 = jnp.bfloat16

D_MODEL = 1024
DEPTH = 4
DA_HEADS = 4
DA_QK_DIM = 64
DA_V_DIM = 128
DA_WIDTH = DA_HEADS * DA_V_DIM
HG_HEADS = 4
HG_K = 128
HG_V = 128
HG_WIDTH = HG_HEADS * HG_V
SG_GROUPS = 8
SG_CHUNK = 128
SG_GROUP_DIM = D_MODEL // SG_GROUPS
REL_BUCKETS = 32
REL_MAX_DIST = 128
D_FF = 2816
EPS = 1e-6

LANES = 128
VMEM_LIMIT = 56 * 1024 * 1024

ROW_TILE = 512
ATTN_TQ = 256
ATTN_TK = 512
HG_BLOCK = 512
HG_CHUNK = 64
HG_LEVELS = 6
FFN_CHUNK = 1408

LOG2E = math.log2(math.e)
Q_SCALE = DA_QK_DIM ** -0.5 * LOG2E


def _cparams(sem):
    return pltpu.CompilerParams(dimension_semantics=sem, vmem_limit_bytes=VMEM_LIMIT)


def _resident(shape):
    nd = len(shape)
    return pl.BlockSpec(shape, lambda *_: (0,) * nd, pipeline_mode=pl.Buffered(1))


def _rms(x, gain):
    ms = jnp.mean(x * x, axis=-1, keepdims=True)
    return x * lax.rsqrt(ms + EPS) * gain


def _silu(x):
    return x / (1.0 + jnp.exp(-x))


def _dot(a, b):
    return jnp.dot(a, b, preferred_element_type=F32)


def _dot_nt(a, b):
    return lax.dot_general(a, b, (((1,), (1,)), ((), ())), preferred_element_type=F32)


def _dot_tn(a, b):
    return lax.dot_general(a, b, (((0,), (0,)), ((), ())), preferred_element_type=F32)


def _band_kernel(rb_ref, o_ref):
    h = pl.program_id(0)
    shape = (LANES, 3 * LANES)
    qi = lax.broadcasted_iota(jnp.int32, shape, 0)
    xi = lax.broadcasted_iota(jnp.int32, shape, 1)
    rel = xi - LANES - qi
    half = REL_BUCKETS // 2
    max_exact = half // 2
    ret = jnp.where(rel > 0, half, 0)
    n = jnp.abs(rel)
    nf = jnp.maximum(n, 1).astype(F32)
    large = max_exact + (jnp.log(nf / max_exact) / math.log(REL_MAX_DIST / max_exact)
                         * (half - max_exact)).astype(jnp.int32)
    large = jnp.minimum(large, half - 1)
    bucket = ret + jnp.where(n < max_exact, n, large)
    out = jnp.zeros(shape, F32)
    for bkt in range(REL_BUCKETS):
        out = jnp.where(bucket == bkt, rb_ref[bkt * DA_HEADS + h], out)
    o_ref[...] = out


def _band_call(rel_bias):
    return pl.pallas_call(
        _band_kernel,
        grid=(DA_HEADS,),
        in_specs=[pl.BlockSpec(memory_space=pltpu.SMEM)],
        out_specs=pl.BlockSpec((None, LANES, 3 * LANES), lambda h: (h, 0, 0)),
        out_shape=jax.ShapeDtypeStruct((DA_HEADS, LANES, 3 * LANES), F32),
        name="rel_bias_band",
    )(rel_bias.reshape(-1))


def _even_in_kernel(x_ref, g_ref, wa_ref, wb_ref, oa_ref, ob_ref):
    hn = _rms(x_ref[...], g_ref[...]).astype(BF16)
    oa = _dot(hn, wa_ref[...])
    oa_ref[:, :DA_WIDTH] = (oa[:, :DA_WIDTH] * Q_SCALE).astype(BF16)
    oa_ref[:, DA_WIDTH:] = oa[:, DA_WIDTH:].astype(BF16)
    ob_ref[...] = _dot(hn, wb_ref[...])


def _even_in_call(h, gain, w_attn, w_hg):
    t = h.shape[0]
    tm = min(ROW_TILE, t)
    na, nb = w_attn.shape[1], w_hg.shape[1]
    return pl.pallas_call(
        _even_in_kernel,
        grid=(t // tm,),
        in_specs=[pl.BlockSpec((tm, D_MODEL), lambda i: (i, 0)),
                  _resident((1, D_MODEL)),
                  _resident((D_MODEL, na)),
                  _resident((D_MODEL, nb))],
        out_specs=[pl.BlockSpec((tm, na), lambda i: (i, 0)),
                   pl.BlockSpec((tm, nb), lambda i: (i, 0))],
        out_shape=[jax.ShapeDtypeStruct((t, na), BF16),
                   jax.ShapeDtypeStruct((t, nb), F32)],
        compiler_params=_cparams(("parallel",)),
        name="even_in_proj",
    )(h, gain, w_attn, w_hg)


def _attn_kernel(rb_ref, lamv_ref, q_ref, k_ref, v_ref, band_ref, sub_ref, o_ref, bias_scr,
                 *, tq, tk, seq, lambda_init):
    h = pl.program_id(0)
    qi = pl.program_id(1)
    b = pl.program_id(2)
    nrb = tq // LANES
    ncb = seq // LANES
    cpb = tk // LANES
    nkb = seq // tk

    @pl.when(b == 0)
    def _():
        far_l = rb_ref[(REL_BUCKETS // 2 - 1) * DA_HEADS + h]
        far_r = rb_ref[(REL_BUCKETS - 1) * DA_HEADS + h]
        for r in range(nrb):
            g = qi * nrb + r
            for c in range(ncb):
                d = jnp.zeros((LANES, LANES), jnp.int32) + (c - g)
                far = jnp.where(d < 0, far_l, far_r)
                tile = jnp.where(d == 0, band_ref[:, LANES:2 * LANES],
                                 jnp.where(d == -1, band_ref[:, 0:LANES],
                                           jnp.where(d == 1, band_ref[:, 2 * LANES:3 * LANES], far)))
                cc = c % cpb
                bias_scr[c // cpb, r * LANES:(r + 1) * LANES, cc * LANES:(cc + 1) * LANES] = (
                    tile * LOG2E)

    q = q_ref[...]
    lane = lax.broadcasted_iota(jnp.int32, q.shape, 1)
    zero = jnp.zeros_like(q)
    qq = jnp.concatenate([jnp.where(lane < DA_QK_DIM, q, zero),
                          jnp.where(lane < DA_QK_DIM, zero, q)], axis=0)

    def scores(kb):
        bt = bias_scr[kb]
        return _dot_nt(qq, k_ref[kb * tk:(kb + 1) * tk, :]) + jnp.concatenate([bt, bt], axis=0)

    m = jnp.full((2 * tq, 1), -jnp.inf, F32)
    l = jnp.zeros((2 * tq, 1), F32)
    acc = jnp.zeros((2 * tq, DA_V_DIM), F32)
    s = scores(0)
    for kb in range(nkb):
        s_next = scores(kb + 1) if kb + 1 < nkb else None
        m_new = jnp.maximum(m, jnp.max(s, axis=-1, keepdims=True))
        alpha = jnp.exp2(m - m_new)
        p = jnp.exp2(s - m_new)
        l = alpha * l + jnp.sum(p, axis=-1, keepdims=True)
        acc = alpha * acc + _dot(p.astype(BF16), v_ref[kb * tk:(kb + 1) * tk, :])
        m = m_new
        s = s_next

    lamv = lamv_ref[...]
    lam = (jnp.exp(jnp.sum(lamv[0:1] * lamv[1:2], axis=-1, keepdims=True))
           - jnp.exp(jnp.sum(lamv[2:3] * lamv[3:4], axis=-1, keepdims=True)) + lambda_init)
    o = acc[:tq] / l[:tq] - lam * (acc[tq:] / l[tq:])
    o_ref[...] = (_rms(o, sub_ref[...]) * (1.0 - lambda_init)).astype(BF16)


def _attn_call(qkv, band, rel_bias, lamv, subln, lambda_init):
    b, s, _ = qkv.shape
    tq = min(ATTN_TQ, s)
    tk = min(ATTN_TK, s)
    kern = functools.partial(_attn_kernel, tq=tq, tk=tk, seq=s, lambda_init=lambda_init)
    return pl.pallas_call(
        kern,
        grid=(DA_HEADS, s // tq, b),
        in_specs=[pl.BlockSpec(memory_space=pltpu.SMEM),
                  _resident((4, DA_QK_DIM)),
                  pl.BlockSpec((None, tq, DA_V_DIM), lambda h, i, bb: (bb, i, h)),
                  pl.BlockSpec((None, s, DA_V_DIM), lambda h, i, bb: (bb, 0, DA_HEADS + h)),
                  pl.BlockSpec((None, s, DA_V_DIM), lambda h, i, bb: (bb, 0, 2 * DA_HEADS + h)),
                  pl.BlockSpec((None, LANES, 3 * LANES), lambda h, i, bb: (h, 0, 0)),
                  _resident((1, DA_V_DIM))],
        out_specs=pl.BlockSpec((None, tq, DA_V_DIM), lambda h, i, bb: (bb, i, h)),
        out_shape=jax.ShapeDtypeStruct((b, s, DA_WIDTH), BF16),
        scratch_shapes=[pltpu.VMEM((s // tk, tq, tk), F32)],
        compiler_params=_cparams(("parallel", "parallel", "arbitrary")),
        name="diff_attention",
    )(rel_bias.reshape(-1), lamv, qkv, qkv, qkv, band, subln)


def _hg_sum_matrix(reverse):
    c = HG_CHUNK
    r = np.arange(c)
    blocks = []
    if not reverse:
        blocks.append(r[None, :] <= r[:, None])
        blocks.append(r[None, :] > r[:, None])
    else:
        blocks.append(r[None, :] >= r[:, None])
        blocks.append(r[None, :] < r[:, None])
    for lvl in range(HG_LEVELS):
        hs = 1 << lvl
        base = r & ~(2 * hs - 1)
        upper = ((r >> lvl) & 1) == 1
        m = np.zeros((c, c), bool)
        for i in range(c):
            if not reverse:
                bnd = base[i] + hs - 1
                lo, hi = (bnd + 1, i) if upper[i] else (i + 1, bnd)
            else:
                bnd = base[i] + hs
                lo, hi = (bnd, i - 1) if upper[i] else (i, bnd - 1)
            m[i, lo:hi + 1] = True
        blocks.append(m)
    blocks.append(np.ones((8, c), bool))
    m = np.concatenate(blocks, axis=0).astype(np.float32)
    return np.concatenate([m, m, m], axis=1)


def _hg_block(q_ref, z_ref, v_ref, lb_ref, m_ref, o_ref, st_ref, nsc, reverse):
    c = HG_CHUNK
    q = _silu(q_ref[...])
    z = z_ref[...]
    v = v_ref[...].astype(BF16)
    lb = lb_ref[...]

    e = jnp.exp(-jnp.abs(z))
    log_sig = jnp.minimum(z, 0.0) - jnp.log1p(e)
    t = jnp.log1p(-lb) + log_sig
    a = jnp.log(lb)
    log_f = jnp.maximum(a, t) + jnp.log1p(jnp.exp(-jnp.abs(a - t)))
    kk = (1.0 - lb) * (jnp.where(z >= 0, e, 1.0) / (1.0 + e))

    def side_by_side(x):
        return jnp.concatenate([x[i * c:(i + 1) * c] for i in range(nsc)], axis=1)

    def chunk(x, i):
        return x[:, i * HG_K:(i + 1) * HG_K]

    l1 = log_f.astype(BF16)
    r1 = log_f - l1.astype(F32)
    l2 = r1.astype(BF16)
    l3 = (r1 - l2.astype(F32)).astype(BF16)
    sums = _dot(m_ref[...], jnp.concatenate(
        [side_by_side(l1), side_by_side(l2), side_by_side(l3)], axis=0))
    cum = sums[0:c]
    rest = sums[c:2 * c]
    tot = sums[(2 + HG_LEVELS) * c:(2 + HG_LEVELS) * c + 1]

    ql = side_by_side(q)
    kl = side_by_side(kk)
    qe = (ql * jnp.exp(cum)).astype(BF16)
    kd = (kl * jnp.exp(rest)).astype(BF16)
    st_decay = jnp.exp(tot)

    row = lax.broadcasted_iota(jnp.int32, ql.shape, 0)
    ri = lax.broadcasted_iota(jnp.int32, (c, c), 0)
    ci = lax.broadcasted_iota(jnp.int32, (c, c), 1)
    qb = ql.astype(BF16)
    kb = kl.astype(BF16)
    att = [jnp.where(ri == ci, _dot_nt(chunk(qb, i), chunk(kb, i)), 0.0) for i in range(nsc)]
    for lvl in range(HG_LEVELS):
        dec = jnp.exp(sums[(2 + lvl) * c:(3 + lvl) * c])
        later = ((row >> lvl) & 1) == 1
        q_rows = jnp.logical_not(later) if reverse else later
        x = jnp.where(q_rows, ql, kl) * dec
        qt = jnp.where(q_rows, x, 0.0).astype(BF16)
        kt = jnp.where(q_rows, 0.0, x).astype(BF16)
        same = (ri >> (lvl + 1)) == (ci >> (lvl + 1))
        att = [att[i] + jnp.where(same, _dot_nt(chunk(qt, i), chunk(kt, i)), 0.0)
               for i in range(nsc)]
    o_intra = [_dot(att[i].astype(BF16), v[i * c:(i + 1) * c]) for i in range(nsc)]
    st_add = [_dot_tn(v[i * c:(i + 1) * c], chunk(kd, i)) for i in range(nsc)]

    st = st_ref[...]
    for i in (reversed(range(nsc)) if reverse else range(nsc)):
        o_ref[i * c:(i + 1) * c, :] = o_intra[i] + _dot_nt(chunk(qe, i), st.astype(BF16))
        st = st * chunk(st_decay, i) + st_add[i]
    st_ref[...] = st


def _hgrn_kernel(qf_ref, zf_ref, vf_ref, qb_ref, zb_ref, vb_ref, lbf_ref, lbb_ref, mf_ref, mb_ref,
                 of_ref, ob_ref, sf_ref, sb_ref, *, tb):
    @pl.when(pl.program_id(2) == 0)
    def _():
        sf_ref[...] = jnp.zeros_like(sf_ref)
        sb_ref[...] = jnp.zeros_like(sb_ref)

    nsc = tb // HG_CHUNK
    _hg_block(qf_ref, zf_ref, vf_ref, lbf_ref, mf_ref, of_ref, sf_ref, nsc, False)
    _hg_block(qb_ref, zb_ref, vb_ref, lbb_ref, mb_ref, ob_ref, sb_ref, nsc, True)


def _hgrn_call(hg, lb_f, lb_b):
    b, s, _ = hg.shape
    tb = min(HG_BLOCK, s)
    nb = s // tb
    hh = HG_HEADS
    mf = jnp.asarray(_hg_sum_matrix(False), BF16)
    mb = jnp.asarray(_hg_sum_matrix(True), BF16)

    def fwd(col):
        return pl.BlockSpec((None, tb, HG_K), lambda bb, h, i: (bb, i, col * hh + h))

    def bwd(col):
        return pl.BlockSpec((None, tb, HG_K), lambda bb, h, i: (bb, nb - 1 - i, col * hh + h))

    lb_spec = pl.BlockSpec((None, 1, HG_K), lambda bb, h, i: (h, 0, 0))
    return pl.pallas_call(
        functools.partial(_hgrn_kernel, tb=tb),
        grid=(b, hh, nb),
        in_specs=[fwd(0), fwd(1), fwd(3), bwd(0), bwd(2), bwd(3), lb_spec, lb_spec,
                  _resident(mf.shape), _resident(mb.shape)],
        out_specs=[pl.BlockSpec((None, tb, HG_V), lambda bb, h, i: (bb, i, h)),
                   pl.BlockSpec((None, tb, HG_V), lambda bb, h, i: (bb, nb - 1 - i, h))],
        out_shape=[jax.ShapeDtypeStruct((b, s, HG_WIDTH), F32),
                   jax.ShapeDtypeStruct((b, s, HG_WIDTH), F32)],
        scratch_shapes=[pltpu.VMEM((HG_V, HG_K), F32), pltpu.VMEM((HG_V, HG_K), F32)],
        compiler_params=_cparams(("parallel", "parallel", "arbitrary")),
        name="hgrn2_bidir",
    )(hg, hg, hg, hg, hg, hg, lb_f.reshape(hh, 1, HG_K), lb_b.reshape(hh, 1, HG_K), mf, mb)


def _even_out_kernel(oa_ref, of_ref, ob_ref, g_ref, h_ref, gn_ref, w_ref, o_ref):
    ob = of_ref[...] + ob_ref[...]
    parts = [oa_ref[...]]
    for hh in range(HG_HEADS):
        cols = slice(hh * HG_V, (hh + 1) * HG_V)
        y = _rms(ob[:, cols], gn_ref[...]) * _silu(g_ref[:, cols])
        parts.append(y.astype(BF16))
    cat = jnp.concatenate(parts, axis=-1)
    o_ref[...] = h_ref[...] + _dot(cat, w_ref[...])


def _even_out_call(o_a, o_f, o_b, hg, h, hg_norm, w_out):
    t = h.shape[0]
    tm = min(ROW_TILE, t)
    gate_blk = hg.shape[1] // HG_WIDTH - 1
    return pl.pallas_call(
        _even_out_kernel,
        grid=(t // tm,),
        in_specs=[pl.BlockSpec((tm, DA_WIDTH), lambda i: (i, 0)),
                  pl.BlockSpec((tm, HG_WIDTH), lambda i: (i, 0)),
                  pl.BlockSpec((tm, HG_WIDTH), lambda i: (i, 0)),
                  pl.BlockSpec((tm, HG_WIDTH), lambda i: (i, gate_blk)),
                  pl.BlockSpec((tm, D_MODEL), lambda i: (i, 0)),
                  _resident((1, HG_V)),
                  _resident(w_out.shape)],
        out_specs=pl.BlockSpec((tm, D_MODEL), lambda i: (i, 0)),
        out_shape=jax.ShapeDtypeStruct((t, D_MODEL), F32),
        compiler_params=_cparams(("parallel",)),
        name="even_out_proj",
    )(o_a, o_f, o_b, hg, h, hg_norm, w_out)


def _odd_kernel(h_ref, g_ref, win_ref, sgn_ref, sgw_ref, sgb_ref, wout_ref, o_ref, vm_ref, *, tm):
    x = h_ref[...]
    hn = _rms(x, g_ref[...]).astype(BF16)
    y = _dot(hn, win_ref[...])
    y = 0.5 * y * (1.0 + lax.erf(y * math.sqrt(0.5)))
    u = y[:, :D_MODEL]
    v = _rms(y[:, D_MODEL:], sgn_ref[...]).astype(BF16)
    for c in range(tm // SG_CHUNK):
        rows = slice(c * SG_CHUNK, (c + 1) * SG_CHUNK)
        for g in range(SG_GROUPS):
            cols = slice(g * SG_GROUP_DIM, (g + 1) * SG_GROUP_DIM)
            vm_ref[rows, cols] = _dot(sgw_ref[g], v[rows, cols]) + sgb_ref[:, cols]
    t = (u * vm_ref[...]).astype(BF16)
    o_ref[...] = x + _dot(t, wout_ref[...])


def _odd_call(h, gain, w_in, sg_norm, sg_w, sg_b_full, w_out):
    t = h.shape[0]
    tm = min(ROW_TILE, t)
    return pl.pallas_call(
        functools.partial(_odd_kernel, tm=tm),
        grid=(t // tm,),
        in_specs=[pl.BlockSpec((tm, D_MODEL), lambda i: (i, 0)),
                  _resident((1, D_MODEL)),
                  _resident(w_in.shape),
                  _resident((1, D_MODEL)),
                  _resident(sg_w.shape),
                  _resident(sg_b_full.shape),
                  _resident(w_out.shape)],
        out_specs=pl.BlockSpec((tm, D_MODEL), lambda i: (i, 0)),
        out_shape=jax.ShapeDtypeStruct((t, D_MODEL), F32),
        scratch_shapes=[pltpu.VMEM((tm, D_MODEL), F32)],
        compiler_params=_cparams(("parallel",)),
        name="odd_mixer",
    )(h, gain, w_in, sg_norm, sg_w, sg_b_full, w_out)


def _ffn_kernel(h_ref, g_ref, win_ref, wout_ref, fg_ref, o_ref, *, final):
    x = h_ref[...]
    hn = _rms(x, g_ref[...]).astype(BF16)
    acc = x
    for c in range(D_FF // FFN_CHUNK):
        lo = c * FFN_CHUNK
        gate = _dot(hn, win_ref[:, lo:lo + FFN_CHUNK])
        up = _dot(hn, win_ref[:, D_FF + lo:D_FF + lo + FFN_CHUNK])
        act = (_silu(gate) * up).astype(BF16)
        acc = acc + _dot(act, wout_ref[lo:lo + FFN_CHUNK, :])
    if final:
        acc = _rms(acc, fg_ref[...])
    o_ref[...] = acc


def _ffn_call(h, gain, w_in, w_out, final_gain, final):
    t = h.shape[0]
    tm = min(ROW_TILE, t)
    return pl.pallas_call(
        functools.partial(_ffn_kernel, final=final),
        grid=(t // tm,),
        in_specs=[pl.BlockSpec((tm, D_MODEL), lambda i: (i, 0)),
                  _resident((1, D_MODEL)),
                  _resident(w_in.shape),
                  _resident(w_out.shape),
                  _resident((1, D_MODEL))],
        out_specs=pl.BlockSpec((tm, D_MODEL), lambda i: (i, 0)),
        out_shape=jax.ShapeDtypeStruct((t, D_MODEL), F32),
        compiler_params=_cparams(("parallel",)),
        name="swiglu_ffn",
    )(h, gain, w_in, w_out, final_gain)


def _lower_bounds(p):
    lb = jnp.cumsum(jax.nn.softmax(p.astype(F32), axis=0), axis=0)
    return lb - lb[:1]


def kernel(x, rel_bias, norm_mix, norm_ffn, norm_final, w_in_even, w_out_even,
           lambda_q1, lambda_k1, lambda_q2, lambda_k2, da_subln, hg_lb_fwd, hg_lb_bwd, hg_norm,
           w_in_odd, sg_norm, sg_w, sg_b, w_out_odd, w_ffn_in, w_ffn_out):
    b, s, d = x.shape
    t = b * s
    lb_f = _lower_bounds(hg_lb_fwd)
    lb_b = _lower_bounds(hg_lb_bwd)
    band = _band_call(rel_bias.astype(F32))
    n_attn = 3 * DA_WIDTH
    h = x.reshape(t, d)
    for l in range(DEPTH):
        gain = norm_mix[l].reshape(1, d)
        if l % 2 == 0:
            e = l // 2
            lambda_init = 0.8 - 0.6 * math.exp(-0.3 * l)
            w_in = w_in_even[e].astype(BF16)
            qkv, hg = _even_in_call(h, gain, w_in[:, :n_attn], w_in[:, n_attn:])
            lamv = jnp.stack([lambda_q1[e], lambda_k1[e], lambda_q2[e], lambda_k2[e]]).astype(F32)
            o_a = _attn_call(qkv.reshape(b, s, n_attn), band, rel_bias.astype(F32), lamv,
                             da_subln[e].reshape(1, DA_V_DIM), lambda_init)
            o_f, o_b = _hgrn_call(hg.reshape(b, s, -1), lb_f[e], lb_b[e])
            h = _even_out_call(o_a.reshape(t, DA_WIDTH), o_f.reshape(t, HG_WIDTH),
                               o_b.reshape(t, HG_WIDTH), hg, h,
                               hg_norm[e].reshape(1, HG_V), w_out_even[e].astype(BF16))
        else:
            o = l // 2
            sg_b_full = jnp.repeat(sg_b[o].T.astype(F32), SG_GROUP_DIM, axis=1)
            h = _odd_call(h, gain, w_in_odd[o].astype(BF16), sg_norm[o].reshape(1, d),
                          sg_w[o].astype(BF16), sg_b_full, w_out_odd[o].astype(BF16))
        h = _ffn_call(h, norm_ffn[l].reshape(1, d), w_ffn_in[l].astype(BF16),
                      w_ffn_out[l].astype(BF16), norm_final.reshape(1, d), l == DEPTH - 1)
    return h.reshape(b, s, d)
```

```python
import functools
import math

import numpy as np
import jax
import jax.numpy as jnp
from jax import lax
from jax.experimental import pallas as pl
from jax.experimental.pallas import tpu as pltpu

F32 = jnp.float32
BF16 = jnp.bfloat16

D_MODEL = 1024
DEPTH = 4
DA_HEADS = 4
DA_QK_DIM = 64
DA_V_DIM = 128
DA_WIDTH = DA_HEADS * DA_V_DIM
HG_HEADS = 4
HG_K = 128
HG_V = 128
HG_WIDTH = HG_HEADS * HG_V
SG_GROUPS = 8
SG_CHUNK = 128
SG_GROUP_DIM = D_MODEL // SG_GROUPS
REL_BUCKETS = 32
REL_MAX_DIST = 128
D_FF = 2816
EPS = 1e-6

LANES = 128
VMEM_LIMIT = 56 * 1024 * 1024

ROW_TILE = 512
ATTN_TQ = 512
ATTN_TK = 512
HG_BLOCK = 512
HG_CHUNK = 64
HG_LEVELS = 6
FFN_CHUNK = 1408

LOG2E = math.log2(math.e)
Q_SCALE = DA_QK_DIM ** -0.5 * LOG2E


def _cparams(sem):
    return pltpu.CompilerParams(dimension_semantics=sem, vmem_limit_bytes=VMEM_LIMIT)


def _resident(shape):
    nd = len(shape)
    return pl.BlockSpec(shape, lambda *_: (0,) * nd, pipeline_mode=pl.Buffered(1))


def _rms(x, gain):
    ms = jnp.mean(x * x, axis=-1, keepdims=True)
    return x * lax.rsqrt(ms + EPS) * gain


def _silu(x):
    return x * pl.reciprocal(1.0 + jnp.exp(-x), approx=True)


def _dot(a, b):
    return jnp.dot(a, b, preferred_element_type=F32)


def _dot_nt(a, b):
    return lax.dot_general(a, b, (((1,), (1,)), ((), ())), preferred_element_type=F32)


def _dot_tn(a, b):
    return lax.dot_general(a, b, (((0,), (0,)), ((), ())), preferred_element_type=F32)


def _band_kernel(rb_ref, o_ref):
    h = pl.program_id(0)
    shape = (LANES, 3 * LANES)
    qi = lax.broadcasted_iota(jnp.int32, shape, 0)
    xi = lax.broadcasted_iota(jnp.int32, shape, 1)
    rel = xi - LANES - qi
    half = REL_BUCKETS // 2
    max_exact = half // 2
    ret = jnp.where(rel > 0, half, 0)
    n = jnp.abs(rel)
    nf = jnp.maximum(n, 1).astype(F32)
    large = max_exact + (jnp.log(nf / max_exact) / math.log(REL_MAX_DIST / max_exact)
                         * (half - max_exact)).astype(jnp.int32)
    large = jnp.minimum(large, half - 1)
    bucket = ret + jnp.where(n < max_exact, n, large)
    out = jnp.zeros(shape, F32)
    for bkt in range(REL_BUCKETS):
        out = jnp.where(bucket == bkt, rb_ref[bkt * DA_HEADS + h], out)
    o_ref[...] = out


def _band_call(rel_bias):
    return pl.pallas_call(
        _band_kernel,
        grid=(DA_HEADS,),
        in_specs=[pl.BlockSpec(memory_space=pltpu.SMEM)],
        out_specs=pl.BlockSpec((None, LANES, 3 * LANES), lambda h: (h, 0, 0)),
        out_shape=jax.ShapeDtypeStruct((DA_HEADS, LANES, 3 * LANES), F32),
        name="rel_bias_band",
    )(rel_bias.reshape(-1))


def _even_in_kernel(x_ref, g_ref, wa_ref, wb_ref, oa_ref, ob_ref):
    hn = _rms(x_ref[...], g_ref[...]).astype(BF16)
    oa = _dot(hn, wa_ref[...])
    oa_ref[:, :DA_WIDTH] = (oa[:, :DA_WIDTH] * Q_SCALE).astype(BF16)
    oa_ref[:, DA_WIDTH:] = oa[:, DA_WIDTH:].astype(BF16)
    ob_ref[...] = _dot(hn, wb_ref[...])


def _even_in_call(h, gain, w_attn, w_hg):
    t = h.shape[0]
    tm = min(ROW_TILE, t)
    na, nb = w_attn.shape[1], w_hg.shape[1]
    return pl.pallas_call(
        _even_in_kernel,
        grid=(t // tm,),
        in_specs=[pl.BlockSpec((tm, D_MODEL), lambda i: (i, 0)),
                  _resident((1, D_MODEL)),
                  _resident((D_MODEL, na)),
                  _resident((D_MODEL, nb))],
        out_specs=[pl.BlockSpec((tm, na), lambda i: (i, 0)),
                   pl.BlockSpec((tm, nb), lambda i: (i, 0))],
        out_shape=[jax.ShapeDtypeStruct((t, na), BF16),
                   jax.ShapeDtypeStruct((t, nb), F32)],
        compiler_params=_cparams(("parallel",)),
        name="even_in_proj",
    )(h, gain, w_attn, w_hg)


def _attn_kernel(rb_ref, lamv_ref, q_ref, k_ref, v_ref, band_ref, sub_ref, o_ref, bias_scr,
                 *, tq, tk, seq, lambda_init):
    h = pl.program_id(0)
    qi = pl.program_id(1)
    b = pl.program_id(2)
    nrb = tq // LANES
    ncb = seq // LANES
    cpb = tk // LANES
    nkb = seq // tk

    @pl.when(b == 0)
    def _():
        far_l = rb_ref[(REL_BUCKETS // 2 - 1) * DA_HEADS + h]
        far_r = rb_ref[(REL_BUCKETS - 1) * DA_HEADS + h]
        for r in range(nrb):
            g = qi * nrb + r
            for c in range(ncb):
                d = jnp.zeros((LANES, LANES), jnp.int32) + (c - g)
                far = jnp.where(d < 0, far_l, far_r)
                tile = jnp.where(d == 0, band_ref[:, LANES:2 * LANES],
                                 jnp.where(d == -1, band_ref[:, 0:LANES],
                                           jnp.where(d == 1, band_ref[:, 2 * LANES:3 * LANES], far)))
                cc = c % cpb
                bias_scr[c // cpb, r * LANES:(r + 1) * LANES, cc * LANES:(cc + 1) * LANES] = (
                    tile * LOG2E)

    q = q_ref[...]
    lane = lax.broadcasted_iota(jnp.int32, q.shape, 1)
    zero = jnp.zeros_like(q)
    qq = jnp.concatenate([jnp.where(lane < DA_QK_DIM, q, zero),
                          jnp.where(lane < DA_QK_DIM, zero, q)], axis=0)

    def scores(kb):
        bt = bias_scr[kb]
        s = _dot_nt(qq, k_ref[kb * tk:(kb + 1) * tk, :]) + jnp.concatenate([bt, bt], axis=0)
        return s.astype(BF16)

    ones = jnp.ones((tk, LANES), BF16)
    m = jnp.full((2 * tq, 1), -jnp.inf, F32)
    acc = jnp.zeros((2 * tq, DA_V_DIM + LANES), F32)
    s = scores(0)
    for kb in range(nkb):
        s_next = scores(kb + 1) if kb + 1 < nkb else None
        m_new = jnp.maximum(m, jnp.max(s, axis=-1, keepdims=True).astype(F32))
        alpha = jnp.exp2(m - m_new)
        p = jnp.exp2(s - m_new.astype(BF16))
        v_ext = jnp.concatenate([v_ref[kb * tk:(kb + 1) * tk, :], ones], axis=1)
        acc = alpha * acc + _dot(p, v_ext)
        m = m_new
        s = s_next

    lamv = lamv_ref[...]
    lam = (jnp.exp(jnp.sum(lamv[0:1] * lamv[1:2], axis=-1, keepdims=True))
           - jnp.exp(jnp.sum(lamv[2:3] * lamv[3:4], axis=-1, keepdims=True)) + lambda_init)
    l = acc[:, DA_V_DIM:DA_V_DIM + 1]
    acc = acc[:, :DA_V_DIM]
    o = acc[:tq] / l[:tq] - lam * (acc[tq:] / l[tq:])
    o_ref[...] = (_rms(o, sub_ref[...]) * (1.0 - lambda_init)).astype(BF16)


def _attn_call(qkv, band, rel_bias, lamv, subln, lambda_init):
    b, s, _ = qkv.shape
    tq = min(ATTN_TQ, s)
    tk = min(ATTN_TK, s)
    kern = functools.partial(_attn_kernel, tq=tq, tk=tk, seq=s, lambda_init=lambda_init)
    return pl.pallas_call(
        kern,
        grid=(DA_HEADS, s // tq, b),
        in_specs=[pl.BlockSpec(memory_space=pltpu.SMEM),
                  _resident((4, DA_QK_DIM)),
                  pl.BlockSpec((None, tq, DA_V_DIM), lambda h, i, bb: (bb, i, h)),
                  pl.BlockSpec((None, s, DA_V_DIM), lambda h, i, bb: (bb, 0, DA_HEADS + h)),
                  pl.BlockSpec((None, s, DA_V_DIM), lambda h, i, bb: (bb, 0, 2 * DA_HEADS + h)),
                  pl.BlockSpec((None, LANES, 3 * LANES), lambda h, i, bb: (h, 0, 0)),
                  _resident((1, DA_V_DIM))],
        out_specs=pl.BlockSpec((None, tq, DA_V_DIM), lambda h, i, bb: (bb, i, h)),
        out_shape=jax.ShapeDtypeStruct((b, s, DA_WIDTH), BF16),
        scratch_shapes=[pltpu.VMEM((s // tk, tq, tk), F32)],
        compiler_params=_cparams(("parallel", "parallel", "arbitrary")),
        name="diff_attention",
    )(rel_bias.reshape(-1), lamv, qkv, qkv, qkv, band, subln)


def _hg_sum_matrix(reverse):
    c = HG_CHUNK
    r = np.arange(c)
    blocks = []
    if not reverse:
        blocks.append(r[None, :] <= r[:, None])
        blocks.append(r[None, :] > r[:, None])
    else:
        blocks.append(r[None, :] >= r[:, None])
        blocks.append(r[None, :] < r[:, None])
    for lvl in range(HG_LEVELS):
        hs = 1 << lvl
        base = r & ~(2 * hs - 1)
        upper = ((r >> lvl) & 1) == 1
        m = np.zeros((c, c), bool)
        for i in range(c):
            if not reverse:
                bnd = base[i] + hs - 1
                lo, hi = (bnd + 1, i) if upper[i] else (i + 1, bnd)
            else:
                bnd = base[i] + hs
                lo, hi = (bnd, i - 1) if upper[i] else (i, bnd - 1)
            m[i, lo:hi + 1] = True
        blocks.append(m)
    blocks.append(np.ones((8, c), bool))
    m = np.concatenate(blocks, axis=0).astype(np.float32)
    return np.concatenate([m, m, m], axis=1)


def _hg_block(q_ref, z_ref, v_ref, lb_ref, m_ref, o_ref, st_ref, nsc, reverse):
    c = HG_CHUNK
    q = _silu(q_ref[...])
    z = z_ref[...]
    v = v_ref[...].astype(BF16)
    lb = lb_ref[...]

    z2 = z * LOG2E
    e = jnp.exp2(-jnp.abs(z2))
    log_sig = jnp.minimum(z2, 0.0) - jnp.log2(1.0 + e)
    t = jnp.log2(1.0 - lb) + log_sig
    a = jnp.log2(lb)
    log_f = jnp.maximum(a, t) + jnp.log2(1.0 + jnp.exp2(-jnp.abs(a - t)))
    kk = (1.0 - lb) * (jnp.where(z >= 0, e, 1.0) * pl.reciprocal(1.0 + e, approx=True))

    def side_by_side(x):
        return jnp.concatenate([x[i * c:(i + 1) * c] for i in range(nsc)], axis=1)

    def chunk(x, i):
        return x[:, i * HG_K:(i + 1) * HG_K]

    l1 = log_f.astype(BF16)
    r1 = log_f - l1.astype(F32)
    l2 = r1.astype(BF16)
    l3 = (r1 - l2.astype(F32)).astype(BF16)
    sums = _dot(m_ref[...], jnp.concatenate(
        [side_by_side(l1), side_by_side(l2), side_by_side(l3)], axis=0))
    cum = sums[0:c]
    rest = sums[c:2 * c]
    tot = sums[(2 + HG_LEVELS) * c:(2 + HG_LEVELS) * c + 1]

    ql = side_by_side(q)
    kl = side_by_side(kk)
    qe = (ql * jnp.exp2(cum)).astype(BF16)
    kd = (kl * jnp.exp2(rest)).astype(BF16)
    st_decay = jnp.exp2(tot)

    row = lax.broadcasted_iota(jnp.int32, ql.shape, 0)
    ri = lax.broadcasted_iota(jnp.int32, (c, c), 0)
    ci = lax.broadcasted_iota(jnp.int32, (c, c), 1)
    qb = ql.astype(BF16)
    kb = kl.astype(BF16)
    att = [jnp.where(ri == ci, _dot_nt(chunk(qb, i), chunk(kb, i)), 0.0) for i in range(nsc)]
    for lvl in range(HG_LEVELS):
        dec = jnp.exp2(sums[(2 + lvl) * c:(3 + lvl) * c])
        later = ((row >> lvl) & 1) == 1
        q_rows = jnp.logical_not(later) if reverse else later
        x = jnp.where(q_rows, ql, kl) * dec
        qt = jnp.where(q_rows, x, 0.0).astype(BF16)
        kt = jnp.where(q_rows, 0.0, x).astype(BF16)
        same = (ri >> (lvl + 1)) == (ci >> (lvl + 1))
        att = [att[i] + jnp.where(same, _dot_nt(chunk(qt, i), chunk(kt, i)), 0.0)
               for i in range(nsc)]
    o_intra = [_dot(att[i].astype(BF16), v[i * c:(i + 1) * c]) for i in range(nsc)]
    st_add = [_dot_tn(v[i * c:(i + 1) * c], chunk(kd, i)) for i in range(nsc)]

    st = st_ref[...]
    for i in (reversed(range(nsc)) if reverse else range(nsc)):
        o_ref[i * c:(i + 1) * c, :] = o_intra[i] + _dot_nt(chunk(qe, i), st.astype(BF16))
        st = st * chunk(st_decay, i) + st_add[i]
    st_ref[...] = st


def _hgrn_kernel(qf_ref, zf_ref, vf_ref, qb_ref, zb_ref, vb_ref, lbf_ref, lbb_ref, mf_ref, mb_ref,
                 of_ref, ob_ref, sf_ref, sb_ref, *, tb):
    @pl.when(pl.program_id(2) == 0)
    def _():
        sf_ref[...] = jnp.zeros_like(sf_ref)
        sb_ref[...] = jnp.zeros_like(sb_ref)

    nsc = tb // HG_CHUNK
    _hg_block(qf_ref, zf_ref, vf_ref, lbf_ref, mf_ref, of_ref, sf_ref, nsc, False)
    _hg_block(qb_ref, zb_ref, vb_ref, lbb_ref, mb_ref, ob_ref, sb_ref, nsc, True)


def _hgrn_call(hg, lb_f, lb_b):
    b, s, _ = hg.shape
    tb = min(HG_BLOCK, s)
    nb = s // tb
    hh = HG_HEADS
    mf = jnp.asarray(_hg_sum_matrix(False), BF16)
    mb = jnp.asarray(_hg_sum_matrix(True), BF16)

    def fwd(col):
        return pl.BlockSpec((None, tb, HG_K), lambda bb, h, i: (bb, i, col * hh + h))

    def bwd(col):
        return pl.BlockSpec((None, tb, HG_K), lambda bb, h, i: (bb, nb - 1 - i, col * hh + h))

    lb_spec = pl.BlockSpec((None, 1, HG_K), lambda bb, h, i: (h, 0, 0))
    return pl.pallas_call(
        functools.partial(_hgrn_kernel, tb=tb),
        grid=(b, hh, nb),
        in_specs=[fwd(0), fwd(1), fwd(3), bwd(0), bwd(2), bwd(3), lb_spec, lb_spec,
                  _resident(mf.shape), _resident(mb.shape)],
        out_specs=[pl.BlockSpec((None, tb, HG_V), lambda bb, h, i: (bb, i, h)),
                   pl.BlockSpec((None, tb, HG_V), lambda bb, h, i: (bb, nb - 1 - i, h))],
        out_shape=[jax.ShapeDtypeStruct((b, s, HG_WIDTH), F32),
                   jax.ShapeDtypeStruct((b, s, HG_WIDTH), F32)],
        scratch_shapes=[pltpu.VMEM((HG_V, HG_K), F32), pltpu.VMEM((HG_V, HG_K), F32)],
        compiler_params=_cparams(("parallel", "parallel", "arbitrary")),
        name="hgrn2_bidir",
    )(hg, hg, hg, hg, hg, hg, lb_f.reshape(hh, 1, HG_K), lb_b.reshape(hh, 1, HG_K), mf, mb)


def _even_out_kernel(oa_ref, of_ref, ob_ref, g_ref, h_ref, gn_ref, w_ref, o_ref):
    ob = of_ref[...] + ob_ref[...]
    parts = [oa_ref[...]]
    for hh in range(HG_HEADS):
        cols = slice(hh * HG_V, (hh + 1) * HG_V)
        y = _rms(ob[:, cols], gn_ref[...]) * _silu(g_ref[:, cols])
        parts.append(y.astype(BF16))
    cat = jnp.concatenate(parts, axis=-1)
    o_ref[...] = h_ref[...] + _dot(cat, w_ref[...])


def _even_out_call(o_a, o_f, o_b, hg, h, hg_norm, w_out):
    t = h.shape[0]
    tm = min(ROW_TILE, t)
    gate_blk = hg.shape[1] // HG_WIDTH - 1
    return pl.pallas_call(
        _even_out_kernel,
        grid=(t // tm,),
        in_specs=[pl.BlockSpec((tm, DA_WIDTH), lambda i: (i, 0)),
                  pl.BlockSpec((tm, HG_WIDTH), lambda i: (i, 0)),
                  pl.BlockSpec((tm, HG_WIDTH), lambda i: (i, 0)),
                  pl.BlockSpec((tm, HG_WIDTH), lambda i: (i, gate_blk)),
                  pl.BlockSpec((tm, D_MODEL), lambda i: (i, 0)),
                  _resident((1, HG_V)),
                  _resident(w_out.shape)],
        out_specs=pl.BlockSpec((tm, D_MODEL), lambda i: (i, 0)),
        out_shape=jax.ShapeDtypeStruct((t, D_MODEL), F32),
        compiler_params=_cparams(("parallel",)),
        name="even_out_proj",
    )(o_a, o_f, o_b, hg, h, hg_norm, w_out)


def _odd_kernel(h_ref, g_ref, win_ref, sgn_ref, sgw_ref, sgb_ref, wout_ref, o_ref, vm_ref, *, tm):
    x = h_ref[...]
    hn = _rms(x, g_ref[...]).astype(BF16)
    y = _dot(hn, win_ref[...])
    y = 0.5 * y * (1.0 + lax.erf(y * math.sqrt(0.5)))
    u = y[:, :D_MODEL]
    v = _rms(y[:, D_MODEL:], sgn_ref[...]).astype(BF16)
    for c in range(tm // SG_CHUNK):
        rows = slice(c * SG_CHUNK, (c + 1) * SG_CHUNK)
        for g in range(SG_GROUPS):
            cols = slice(g * SG_GROUP_DIM, (g + 1) * SG_GROUP_DIM)
            vm_ref[rows, cols] = _dot(sgw_ref[g], v[rows, cols]) + sgb_ref[:, cols]
    t = (u * vm_ref[...]).astype(BF16)
    o_ref[...] = x + _dot(t, wout_ref[...])


def _odd_call(h, gain, w_in, sg_norm, sg_w, sg_b_full, w_out):
    t = h.shape[0]
    tm = min(ROW_TILE, t)
    return pl.pallas_call(
        functools.partial(_odd_kernel, tm=tm),
        grid=(t // tm,),
        in_specs=[pl.BlockSpec((tm, D_MODEL), lambda i: (i, 0)),
                  _resident((1, D_MODEL)),
                  _resident(w_in.shape),
                  _resident((1, D_MODEL)),
                  _resident(sg_w.shape),
                  _resident(sg_b_full.shape),
                  _resident(w_out.shape)],
        out_specs=pl.BlockSpec((tm, D_MODEL), lambda i: (i, 0)),
        out_shape=jax.ShapeDtypeStruct((t, D_MODEL), F32),
        scratch_shapes=[pltpu.VMEM((tm, D_MODEL), F32)],
        compiler_params=_cparams(("parallel",)),
        name="odd_mixer",
    )(h, gain, w_in, sg_norm, sg_w, sg_b_full, w_out)


def _ffn_kernel(h_ref, g_ref, win_ref, wout_ref, fg_ref, o_ref, *, final):
    x = h_ref[...]
    hn = _rms(x, g_ref[...]).astype(BF16)
    acc = x
    for c in range(D_FF // FFN_CHUNK):
        lo = c * FFN_CHUNK
        gate = _dot(hn, win_ref[:, lo:lo + FFN_CHUNK])
        up = _dot(hn, win_ref[:, D_FF + lo:D_FF + lo + FFN_CHUNK])
        act = (_silu(gate) * up).astype(BF16)
        acc = acc + _dot(act, wout_ref[lo:lo + FFN_CHUNK, :])
    if final:
        acc = _rms(acc, fg_ref[...])
    o_ref[...] = acc


def _ffn_call(h, gain, w_in, w_out, final_gain, final):
    t = h.shape[0]
    tm = min(ROW_TILE, t)
    return pl.pallas_call(
        functools.partial(_ffn_kernel, final=final),
        grid=(t // tm,),
        in_specs=[pl.BlockSpec((tm, D_MODEL), lambda i: (i, 0)),
                  _resident((1, D_MODEL)),
                  _resident(w_in.shape),
                  _resident(w_out.shape),
                  _resident((1, D_MODEL))],
        out_specs=pl.BlockSpec((tm, D_MODEL), lambda i: (i, 0)),
        out_shape=jax.ShapeDtypeStruct((t, D_MODEL), F32),
        compiler_params=_cparams(("parallel",)),
        name="swiglu_ffn",
    )(h, gain, w_in, w_out, final_gain)


def _lower_bounds(p):
    lb = jnp.cumsum(jax.nn.softmax(p.astype(F32), axis=0), axis=0)
    return lb - lb[:1]


def kernel(x, rel_bias, norm_mix, norm_ffn, norm_final, w_in_even, w_out_even,
           lambda_q1, lambda_k1, lambda_q2, lambda_k2, da_subln, hg_lb_fwd, hg_lb_bwd, hg_norm,
           w_in_odd, sg_norm, sg_w, sg_b, w_out_odd, w_ffn_in, w_ffn_out):
    b, s, d = x.shape
    t = b * s
    lb_f = _lower_bounds(hg_lb_fwd)
    lb_b = _lower_bounds(hg_lb_bwd)
    band = _band_call(rel_bias.astype(F32))
    n_attn = 3 * DA_WIDTH
    h = x.reshape(t, d)
    for l in range(DEPTH):
        gain = norm_mix[l].reshape(1, d)
        if l % 2 == 0:
            e = l // 2
            lambda_init = 0.8 - 0.6 * math.exp(-0.3 * l)
            w_in = w_in_even[e].astype(BF16)
            qkv, hg = _even_in_call(h, gain, w_in[:, :n_attn], w_in[:, n_attn:])
            lamv = jnp.stack([lambda_q1[e], lambda_k1[e], lambda_q2[e], lambda_k2[e]]).astype(F32)
            o_a = _attn_call(qkv.reshape(b, s, n_attn), band, rel_bias.astype(F32), lamv,
                             da_subln[e].reshape(1, DA_V_DIM), lambda_init)
            o_f, o_b = _hgrn_call(hg.reshape(b, s, -1), lb_f[e], lb_b[e])
            h = _even_out_call(o_a.reshape(t, DA_WIDTH), o_f.reshape(t, HG_WIDTH),
                               o_b.reshape(t, HG_WIDTH), hg, h,
                               hg_norm[e].reshape(1, HG_V), w_out_even[e].astype(BF16))
        else:
            o = l // 2
            sg_b_full = jnp.repeat(sg_b[o].T.astype(F32), SG_GROUP_DIM, axis=1)
            h = _odd_call(h, gain, w_in_odd[o].astype(BF16), sg_norm[o].reshape(1, d),
                          sg_w[o].astype(BF16), sg_b_full, w_out_odd[o].astype(BF16))
        h = _ffn_call(h, norm_ffn[l].reshape(1, d), w_ffn_in[l].astype(BF16),
                      w_ffn_out[l].astype(BF16), norm_final.reshape(1, d), l == DEPTH - 1)
    return h.reshape(b, s, d)
```

```python
import functools
import math

import numpy as np
import jax
import jax.numpy as jnp
from jax import lax
from jax.experimental import pallas as pl
from jax.experimental.pallas import tpu as pltpu

F32 = jnp.float32
BF16 = jnp.bfloat16

D_MODEL = 1024
DEPTH = 4
DA_HEADS = 4
DA_QK_DIM = 64
DA_V_DIM = 128
DA_WIDTH = DA_HEADS * DA_V_DIM
HG_HEADS = 4
HG_K = 128
HG_V = 128
HG_WIDTH = HG_HEADS * HG_V
SG_GROUPS = 8
SG_CHUNK = 128
SG_GROUP_DIM = D_MODEL // SG_GROUPS
REL_BUCKETS = 32
REL_MAX_DIST = 128
D_FF = 2816
EPS = 1e-6

LANES = 128
VMEM_LIMIT = 56 * 1024 * 1024

ROW_TILE = 512
ATTN_TQ = 512
ATTN_TK = 512
HG_BLOCK = 512
HG_CHUNK = 64
HG_LEVELS = 6
HG_TOT_ROWS = 16
HG_MAIN_ROWS = 2 * HG_CHUNK + HG_TOT_ROWS
HG_FAST_LOG2_LIMIT = -100.0
FFN_CHUNK = 2816

LOG2E = math.log2(math.e)
Q_SCALE = DA_QK_DIM ** -0.5 * LOG2E


def _cparams(sem):
    return pltpu.CompilerParams(dimension_semantics=sem, vmem_limit_bytes=VMEM_LIMIT)


def _resident(shape):
    nd = len(shape)
    return pl.BlockSpec(shape, lambda *_: (0,) * nd, pipeline_mode=pl.Buffered(1))


def _rms(x, gain):
    ms = jnp.mean(x * x, axis=-1, keepdims=True)
    return x * lax.rsqrt(ms + EPS) * gain


def _silu(x):
    return x * pl.reciprocal(1.0 + jnp.exp(-x), approx=True)


def _dot(a, b):
    return jnp.dot(a, b, preferred_element_type=F32)


def _dot_nt(a, b):
    return lax.dot_general(a, b, (((1,), (1,)), ((), ())), preferred_element_type=F32)


def _dot_tn(a, b):
    return lax.dot_general(a, b, (((0,), (0,)), ((), ())), preferred_element_type=F32)


def _band_kernel(rb_ref, o_ref):
    h = pl.program_id(0)
    shape = (LANES, 3 * LANES)
    qi = lax.broadcasted_iota(jnp.int32, shape, 0)
    xi = lax.broadcasted_iota(jnp.int32, shape, 1)
    rel = xi - LANES - qi
    half = REL_BUCKETS // 2
    max_exact = half // 2
    ret = jnp.where(rel > 0, half, 0)
    n = jnp.abs(rel)
    nf = jnp.maximum(n, 1).astype(F32)
    large = max_exact + (jnp.log(nf / max_exact) / math.log(REL_MAX_DIST / max_exact)
                         * (half - max_exact)).astype(jnp.int32)
    large = jnp.minimum(large, half - 1)
    bucket = ret + jnp.where(n < max_exact, n, large)
    out = jnp.zeros(shape, F32)
    for bkt in range(REL_BUCKETS):
        out = jnp.where(bucket == bkt, rb_ref[bkt * DA_HEADS + h], out)
    o_ref[...] = out


def _band_call(rel_bias):
    return pl.pallas_call(
        _band_kernel,
        grid=(DA_HEADS,),
        in_specs=[pl.BlockSpec(memory_space=pltpu.SMEM)],
        out_specs=pl.BlockSpec((None, LANES, 3 * LANES), lambda h: (h, 0, 0)),
        out_shape=jax.ShapeDtypeStruct((DA_HEADS, LANES, 3 * LANES), F32),
        name="rel_bias_band",
    )(rel_bias.reshape(-1))


def _even_in_kernel(x_ref, g_ref, wa_ref, wb_ref, oa_ref, ob_ref):
    hn = _rms(x_ref[...], g_ref[...]).astype(BF16)
    oa = _dot(hn, wa_ref[...])
    oa_ref[:, :DA_WIDTH] = (oa[:, :DA_WIDTH] * Q_SCALE).astype(BF16)
    oa_ref[:, DA_WIDTH:] = oa[:, DA_WIDTH:].astype(BF16)
    ob = _dot(hn, wb_ref[...])
    ob_ref[:, :HG_WIDTH] = _silu(ob[:, :HG_WIDTH])
    ob_ref[:, HG_WIDTH:] = ob[:, HG_WIDTH:]


def _even_in_call(h, gain, w_attn, w_hg):
    t = h.shape[0]
    tm = min(ROW_TILE, t)
    na, nb = w_attn.shape[1], w_hg.shape[1]
    return pl.pallas_call(
        _even_in_kernel,
        grid=(t // tm,),
        in_specs=[pl.BlockSpec((tm, D_MODEL), lambda i: (i, 0)),
                  _resident((1, D_MODEL)),
                  _resident((D_MODEL, na)),
                  _resident((D_MODEL, nb))],
        out_specs=[pl.BlockSpec((tm, na), lambda i: (i, 0)),
                   pl.BlockSpec((tm, nb), lambda i: (i, 0))],
        out_shape=[jax.ShapeDtypeStruct((t, na), BF16),
                   jax.ShapeDtypeStruct((t, nb), F32)],
        compiler_params=_cparams(("parallel",)),
        name="even_in_proj",
    )(h, gain, w_attn, w_hg)


def _attn_kernel(rb_ref, lamv_ref, q_ref, k_ref, v_ref, band_ref, sub_ref, o_ref, bias_scr,
                 *, tq, tk, seq, lambda_init):
    h = pl.program_id(0)
    qi = pl.program_id(1)
    b = pl.program_id(2)
    nrb = tq // LANES
    ncb = seq // LANES
    cpb = tk // LANES
    nkb = seq // tk

    @pl.when(b == 0)
    def _():
        far_l = rb_ref[(REL_BUCKETS // 2 - 1) * DA_HEADS + h]
        far_r = rb_ref[(REL_BUCKETS - 1) * DA_HEADS + h]
        for r in range(nrb):
            g = qi * nrb + r
            for c in range(ncb):
                d = jnp.zeros((LANES, LANES), jnp.int32) + (c - g)
                far = jnp.where(d < 0, far_l, far_r)
                tile = jnp.where(d == 0, band_ref[:, LANES:2 * LANES],
                                 jnp.where(d == -1, band_ref[:, 0:LANES],
                                           jnp.where(d == 1, band_ref[:, 2 * LANES:3 * LANES], far)))
                cc = c % cpb
                bias_scr[c // cpb, r * LANES:(r + 1) * LANES, cc * LANES:(cc + 1) * LANES] = (
                    tile * LOG2E)

    q = q_ref[...]
    lane = lax.broadcasted_iota(jnp.int32, q.shape, 1)
    zero = jnp.zeros_like(q)
    qq = jnp.concatenate([jnp.where(lane < DA_QK_DIM, q, zero),
                          jnp.where(lane < DA_QK_DIM, zero, q)], axis=0)

    def scores(kb):
        bt = bias_scr[kb]
        s = _dot_nt(qq, k_ref[kb * tk:(kb + 1) * tk, :]) + jnp.concatenate([bt, bt], axis=0)
        return s.astype(BF16)

    ones = jnp.ones((tk, LANES), BF16)
    m = jnp.full((2 * tq, 1), -jnp.inf, F32)
    acc = jnp.zeros((2 * tq, DA_V_DIM + LANES), F32)
    s = scores(0)
    for kb in range(nkb):
        s_next = scores(kb + 1) if kb + 1 < nkb else None
        m_new = jnp.maximum(m, jnp.max(s, axis=-1, keepdims=True).astype(F32))
        alpha = jnp.exp2(m - m_new)
        p = jnp.exp2(s - m_new.astype(BF16))
        v_ext = jnp.concatenate([v_ref[kb * tk:(kb + 1) * tk, :], ones], axis=1)
        acc = alpha * acc + _dot(p, v_ext)
        m = m_new
        s = s_next

    lamv = lamv_ref[...]
    lam = (jnp.exp(jnp.sum(lamv[0:1] * lamv[1:2], axis=-1, keepdims=True))
           - jnp.exp(jnp.sum(lamv[2:3] * lamv[3:4], axis=-1, keepdims=True)) + lambda_init)
    l = acc[:, DA_V_DIM:DA_V_DIM + 1]
    acc = acc[:, :DA_V_DIM]
    o = acc[:tq] / l[:tq] - lam * (acc[tq:] / l[tq:])
    o_ref[...] = (_rms(o, sub_ref[...]) * (1.0 - lambda_init)).astype(BF16)


def _attn_call(qkv, band, rel_bias, lamv, subln, lambda_init):
    b, s, _ = qkv.shape
    tq = min(ATTN_TQ, s)
    tk = min(ATTN_TK, s)
    kern = functools.partial(_attn_kernel, tq=tq, tk=tk, seq=s, lambda_init=lambda_init)
    return pl.pallas_call(
        kern,
        grid=(DA_HEADS, s // tq, b),
        in_specs=[pl.BlockSpec(memory_space=pltpu.SMEM),
                  _resident((4, DA_QK_DIM)),
                  pl.BlockSpec((None, tq, DA_V_DIM), lambda h, i, bb: (bb, i, h)),
                  pl.BlockSpec((None, s, DA_V_DIM), lambda h, i, bb: (bb, 0, DA_HEADS + h)),
                  pl.BlockSpec((None, s, DA_V_DIM), lambda h, i, bb: (bb, 0, 2 * DA_HEADS + h)),
                  pl.BlockSpec((None, LANES, 3 * LANES), lambda h, i, bb: (h, 0, 0)),
                  _resident((1, DA_V_DIM))],
        out_specs=pl.BlockSpec((None, tq, DA_V_DIM), lambda h, i, bb: (bb, i, h)),
        out_shape=jax.ShapeDtypeStruct((b, s, DA_WIDTH), BF16),
        scratch_shapes=[pltpu.VMEM((s // tk, tq, tk), F32)],
        compiler_params=_cparams(("parallel", "parallel", "arbitrary")),
        name="diff_attention",
    )(rel_bias.reshape(-1), lamv, qkv, qkv, qkv, band, subln)


def _hg_sum_matrix(reverse):
    c = HG_CHUNK
    r = np.arange(c)
    blocks = []
    if not reverse:
        blocks.append(r[None, :] <= r[:, None])
        blocks.append(r[None, :] > r[:, None])
    else:
        blocks.append(r[None, :] >= r[:, None])
        blocks.append(r[None, :] < r[:, None])
    blocks.append(np.ones((HG_TOT_ROWS, c), bool))
    for lvl in range(HG_LEVELS):
        hs = 1 << lvl
        base = r & ~(2 * hs - 1)
        upper = ((r >> lvl) & 1) == 1
        m = np.zeros((c, c), bool)
        for i in range(c):
            if not reverse:
                bnd = base[i] + hs - 1
                lo, hi = (bnd + 1, i) if upper[i] else (i + 1, bnd)
            else:
                bnd = base[i] + hs
                lo, hi = (bnd, i - 1) if upper[i] else (i, bnd - 1)
            m[i, lo:hi + 1] = True
        blocks.append(m)
    m = np.concatenate(blocks, axis=0).astype(np.float32)
    return np.concatenate([m, m, m], axis=1)


def _hg_block(q_ref, z_ref, v_ref, lb_ref, m_ref, o_ref, st_ref, inter_ref, nsc, reverse):
    c = HG_CHUNK
    q = q_ref[...]
    z = z_ref[...]
    v = v_ref[...].astype(BF16)
    lb = lb_ref[...]

    z2 = z * LOG2E
    e = jnp.exp2(-jnp.abs(z2))
    r = pl.reciprocal(1.0 + e, approx=True)
    pos = z >= 0
    sig = jnp.where(pos, r, e * r)
    kk = (1.0 - lb) * jnp.where(pos, e * r, r)
    has_lb = lb > 0
    log_f = (jnp.where(has_lb, 0.0, jnp.minimum(z2, 0.0))
             + jnp.log2(jnp.where(has_lb, lb + (1.0 - lb) * sig, r)))

    def side_by_side(x):
        return jnp.concatenate([x[i * c:(i + 1) * c] for i in range(nsc)], axis=1)

    def chunk(x, i):
        return x[:, i * HG_K:(i + 1) * HG_K]

    l1 = log_f.astype(BF16)
    r1 = log_f - l1.astype(F32)
    l2 = r1.astype(BF16)
    l3 = (r1 - l2.astype(F32)).astype(BF16)
    lcat = jnp.concatenate([side_by_side(l1), side_by_side(l2), side_by_side(l3)], axis=0)
    sums = _dot(m_ref[0:HG_MAIN_ROWS, :], lcat)
    cum = sums[0:c]
    rest = sums[c:2 * c]
    tot = sums[2 * c:2 * c + 1]

    ql = side_by_side(q)
    kl = side_by_side(kk)
    qe = (ql * jnp.exp2(cum)).astype(BF16)
    kd = (kl * jnp.exp2(rest)).astype(BF16)
    st_decay = jnp.exp2(tot)

    ri = lax.broadcasted_iota(jnp.int32, (c, c), 0)
    ci = lax.broadcasted_iota(jnp.int32, (c, c), 1)

    mid = cum[c // 2:c // 2 + 1]
    qx = (ql * jnp.exp2(cum - mid)).astype(BF16)
    kx = (kl * jnp.exp2(mid - cum)).astype(BF16)
    causal = (ri <= ci) if reverse else (ri >= ci)
    o_intra = [_dot(jnp.where(causal, _dot_nt(chunk(qx, i), chunk(kx, i)), 0.0).astype(BF16),
                    v[i * c:(i + 1) * c]) for i in range(nsc)]

    st_add = [_dot_tn(v[i * c:(i + 1) * c], chunk(kd, i)) for i in range(nsc)]
    st = st_ref[...]
    for i in (reversed(range(nsc)) if reverse else range(nsc)):
        inter = _dot_nt(chunk(qe, i), st.astype(BF16))
        inter_ref[i * c:(i + 1) * c, :] = inter
        o_ref[i * c:(i + 1) * c, :] = o_intra[i] + inter
        st = st * chunk(st_decay, i) + st_add[i]
    st_ref[...] = st

    def redo_if_unsafe():
        @pl.when(jnp.min(tot) < HG_FAST_LOG2_LIMIT)
        def _():
            lsum = _dot(m_ref[HG_MAIN_ROWS:HG_MAIN_ROWS + HG_LEVELS * c, :], lcat)
            row = lax.broadcasted_iota(jnp.int32, ql.shape, 0)
            qb = ql.astype(BF16)
            kb = kl.astype(BF16)
            att = [jnp.where(ri == ci, _dot_nt(chunk(qb, i), chunk(kb, i)), 0.0)
                   for i in range(nsc)]
            for lvl in range(HG_LEVELS):
                dec = jnp.exp2(lsum[lvl * c:(lvl + 1) * c])
                later = ((row >> lvl) & 1) == 1
                q_rows = jnp.logical_not(later) if reverse else later
                x = jnp.where(q_rows, ql, kl) * dec
                qt = jnp.where(q_rows, x, 0.0).astype(BF16)
                kt = jnp.where(q_rows, 0.0, x).astype(BF16)
                same = (ri >> (lvl + 1)) == (ci >> (lvl + 1))
                att = [att[i] + jnp.where(same, _dot_nt(chunk(qt, i), chunk(kt, i)), 0.0)
                       for i in range(nsc)]
            for i in range(nsc):
                rows = slice(i * c, (i + 1) * c)
                o_ref[rows, :] = _dot(att[i].astype(BF16), v[rows]) + inter_ref[rows, :]

    return redo_if_unsafe


def _hgrn_kernel(qf_ref, zf_ref, vf_ref, qb_ref, zb_ref, vb_ref, lbf_ref, lbb_ref, mf_ref, mb_ref,
                 of_ref, ob_ref, sf_ref, sb_ref, if_ref, ib_ref, *, tb):
    @pl.when(pl.program_id(2) == 0)
    def _():
        sf_ref[...] = jnp.zeros_like(sf_ref)
        sb_ref[...] = jnp.zeros_like(sb_ref)

    nsc = tb // HG_CHUNK
    redo_f = _hg_block(qf_ref, zf_ref, vf_ref, lbf_ref, mf_ref, of_ref, sf_ref, if_ref, nsc, False)
    redo_b = _hg_block(qb_ref, zb_ref, vb_ref, lbb_ref, mb_ref, ob_ref, sb_ref, ib_ref, nsc, True)
    redo_f()
    redo_b()


def _hgrn_call(hg, lb_f, lb_b):
    b, s, _ = hg.shape
    tb = min(HG_BLOCK, s)
    nb = s // tb
    hh = HG_HEADS
    mf = jnp.asarray(_hg_sum_matrix(False), BF16)
    mb = jnp.asarray(_hg_sum_matrix(True), BF16)

    def fwd(col):
        return pl.BlockSpec((None, tb, HG_K), lambda bb, h, i: (bb, i, col * hh + h))

    def bwd(col):
        return pl.BlockSpec((None, tb, HG_K), lambda bb, h, i: (bb, nb - 1 - i, col * hh + h))

    lb_spec = pl.BlockSpec((None, 1, HG_K), lambda bb, h, i: (h, 0, 0))
    return pl.pallas_call(
        functools.partial(_hgrn_kernel, tb=tb),
        grid=(b, hh, nb),
        in_specs=[fwd(0), fwd(1), fwd(3), bwd(0), bwd(2), bwd(3), lb_spec, lb_spec,
                  _resident(mf.shape), _resident(mb.shape)],
        out_specs=[pl.BlockSpec((None, tb, HG_V), lambda bb, h, i: (bb, i, h)),
                   pl.BlockSpec((None, tb, HG_V), lambda bb, h, i: (bb, nb - 1 - i, h))],
        out_shape=[jax.ShapeDtypeStruct((b, s, HG_WIDTH), F32),
                   jax.ShapeDtypeStruct((b, s, HG_WIDTH), F32)],
        scratch_shapes=[pltpu.VMEM((HG_V, HG_K), F32), pltpu.VMEM((HG_V, HG_K), F32),
                        pltpu.VMEM((tb, HG_V), F32), pltpu.VMEM((tb, HG_V), F32)],
        compiler_params=_cparams(("parallel", "parallel", "arbitrary")),
        name="hgrn2_bidir",
    )(hg, hg, hg, hg, hg, hg, lb_f.reshape(hh, 1, HG_K), lb_b.reshape(hh, 1, HG_K), mf, mb)


def _even_out_kernel(oa_ref, of_ref, ob_ref, g_ref, h_ref, gn_ref, w_ref, o_ref):
    ob = of_ref[...] + ob_ref[...]
    parts = [oa_ref[...]]
    for hh in range(HG_HEADS):
        cols = slice(hh * HG_V, (hh + 1) * HG_V)
        y = _rms(ob[:, cols], gn_ref[...]) * _silu(g_ref[:, cols])
        parts.append(y.astype(BF16))
    cat = jnp.concatenate(parts, axis=-1)
    o_ref[...] = h_ref[...] + _dot(cat, w_ref[...])


def _even_out_call(o_a, o_f, o_b, hg, h, hg_norm, w_out):
    t = h.shape[0]
    tm = min(ROW_TILE, t)
    gate_blk = hg.shape[1] // HG_WIDTH - 1
    return pl.pallas_call(
        _even_out_kernel,
        grid=(t // tm,),
        in_specs=[pl.BlockSpec((tm, DA_WIDTH), lambda i: (i, 0)),
                  pl.BlockSpec((tm, HG_WIDTH), lambda i: (i, 0)),
                  pl.BlockSpec((tm, HG_WIDTH), lambda i: (i, 0)),
                  pl.BlockSpec((tm, HG_WIDTH), lambda i: (i, gate_blk)),
                  pl.BlockSpec((tm, D_MODEL), lambda i: (i, 0)),
                  _resident((1, HG_V)),
                  _resident(w_out.shape)],
        out_specs=pl.BlockSpec((tm, D_MODEL), lambda i: (i, 0)),
        out_shape=jax.ShapeDtypeStruct((t, D_MODEL), F32),
        compiler_params=_cparams(("parallel",)),
        name="even_out_proj",
    )(o_a, o_f, o_b, hg, h, hg_norm, w_out)


def _odd_kernel(h_ref, g_ref, win_ref, sgn_ref, sgw_ref, sgb_ref, wout_ref, o_ref, vm_ref, *, tm):
    x = h_ref[...]
    hn = _rms(x, g_ref[...]).astype(BF16)
    y = _dot(hn, win_ref[...])
    y = 0.5 * y * (1.0 + lax.erf(y * math.sqrt(0.5)))
    u = y[:, :D_MODEL]
    v = _rms(y[:, D_MODEL:], sgn_ref[...]).astype(BF16)
    for c in range(tm // SG_CHUNK):
        rows = slice(c * SG_CHUNK, (c + 1) * SG_CHUNK)
        for g in range(SG_GROUPS):
            cols = slice(g * SG_GROUP_DIM, (g + 1) * SG_GROUP_DIM)
            vm_ref[rows, cols] = _dot(sgw_ref[g], v[rows, cols]) + sgb_ref[:, cols]
    t = (u * vm_ref[...]).astype(BF16)
    o_ref[...] = x + _dot(t, wout_ref[...])


def _odd_call(h, gain, w_in, sg_norm, sg_w, sg_b_full, w_out):
    t = h.shape[0]
    tm = min(ROW_TILE, t)
    return pl.pallas_call(
        functools.partial(_odd_kernel, tm=tm),
        grid=(t // tm,),
        in_specs=[pl.BlockSpec((tm, D_MODEL), lambda i: (i, 0)),
                  _resident((1, D_MODEL)),
                  _resident(w_in.shape),
                  _resident((1, D_MODEL)),
                  _resident(sg_w.shape),
                  _resident(sg_b_full.shape),
                  _resident(w_out.shape)],
        out_specs=pl.BlockSpec((tm, D_MODEL), lambda i: (i, 0)),
        out_shape=jax.ShapeDtypeStruct((t, D_MODEL), F32),
        scratch_shapes=[pltpu.VMEM((tm, D_MODEL), F32)],
        compiler_params=_cparams(("parallel",)),
        name="odd_mixer",
    )(h, gain, w_in, sg_norm, sg_w, sg_b_full, w_out)


def _ffn_kernel(h_ref, g_ref, win_ref, wout_ref, fg_ref, o_ref, *, final):
    x = h_ref[...]
    hn = _rms(x, g_ref[...]).astype(BF16)
    acc = x
    for c in range(D_FF // FFN_CHUNK):
        lo = c * FFN_CHUNK
        gate = _dot(hn, win_ref[:, lo:lo + FFN_CHUNK])
        up = _dot(hn, win_ref[:, D_FF + lo:D_FF + lo + FFN_CHUNK])
        act = (_silu(gate) * up).astype(BF16)
        acc = acc + _dot(act, wout_ref[lo:lo + FFN_CHUNK, :])
    if final:
        acc = _rms(acc, fg_ref[...])
    o_ref[...] = acc


def _ffn_call(h, gain, w_in, w_out, final_gain, final):
    t = h.shape[0]
    tm = min(ROW_TILE, t)
    return pl.pallas_call(
        functools.partial(_ffn_kernel, final=final),
        grid=(t // tm,),
        in_specs=[pl.BlockSpec((tm, D_MODEL), lambda i: (i, 0)),
                  _resident((1, D_MODEL)),
                  _resident(w_in.shape),
                  _resident(w_out.shape),
                  _resident((1, D_MODEL))],
        out_specs=pl.BlockSpec((tm, D_MODEL), lambda i: (i, 0)),
        out_shape=jax.ShapeDtypeStruct((t, D_MODEL), F32),
        compiler_params=_cparams(("parallel",)),
        name="swiglu_ffn",
    )(h, gain, w_in, w_out, final_gain)


def _lower_bounds(p):
    lb = jnp.cumsum(jax.nn.softmax(p.astype(F32), axis=0), axis=0)
    return lb - lb[:1]


def kernel(x, rel_bias, norm_mix, norm_ffn, norm_final, w_in_even, w_out_even,
           lambda_q1, lambda_k1, lambda_q2, lambda_k2, da_subln, hg_lb_fwd, hg_lb_bwd, hg_norm,
           w_in_odd, sg_norm, sg_w, sg_b, w_out_odd, w_ffn_in, w_ffn_out):
    b, s, d = x.shape
    t = b * s
    lb_f = _lower_bounds(hg_lb_fwd)
    lb_b = _lower_bounds(hg_lb_bwd)
    band = _band_call(rel_bias.astype(F32))
    n_attn = 3 * DA_WIDTH
    h = x.reshape(t, d)
    for l in range(DEPTH):
        gain = norm_mix[l].reshape(1, d)
        if l % 2 == 0:
            e = l // 2
            lambda_init = 0.8 - 0.6 * math.exp(-0.3 * l)
            w_in = w_in_even[e].astype(BF16)
            qkv, hg = _even_in_call(h, gain, w_in[:, :n_attn], w_in[:, n_attn:])
            lamv = jnp.stack([lambda_q1[e], lambda_k1[e], lambda_q2[e], lambda_k2[e]]).astype(F32)
            o_a = _attn_call(qkv.reshape(b, s, n_attn), band, rel_bias.astype(F32), lamv,
                             da_subln[e].reshape(1, DA_V_DIM), lambda_init)
            o_f, o_b = _hgrn_call(hg.reshape(b, s, -1), lb_f[e], lb_b[e])
            h = _even_out_call(o_a.reshape(t, DA_WIDTH), o_f.reshape(t, HG_WIDTH),
                               o_b.reshape(t, HG_WIDTH), hg, h,
                               hg_norm[e].reshape(1, HG_V), w_out_even[e].astype(BF16))
        else:
            o = l // 2
            sg_b_full = jnp.repeat(sg_b[o].T.astype(F32), SG_GROUP_DIM, axis=1)
            h = _odd_call(h, gain, w_in_odd[o].astype(BF16), sg_norm[o].reshape(1, d),
                          sg_w[o].astype(BF16), sg_b_full, w_out_odd[o].astype(BF16))
        h = _ffn_call(h, norm_ffn[l].reshape(1, d), w_ffn_in[l].astype(BF16),
                      w_ffn_out[l].astype(BF16), norm_final.reshape(1, d), l == DEPTH - 1)
    return h.reshape(b, s, d)
```

```python
import functools
import math

import numpy as np
import jax
import jax.numpy as jnp
from jax import lax
from jax.experimental import pallas as pl
from jax.experimental.pallas import tpu as pltpu

F32 = jnp.float32
BF16 = jnp.bfloat16

D_MODEL = 1024
DEPTH = 4
DA_HEADS = 4
DA_QK_DIM = 64
DA_V_DIM = 128
DA_WIDTH = DA_HEADS * DA_V_DIM
HG_HEADS = 4
HG_K = 128
HG_V = 128
HG_WIDTH = HG_HEADS * HG_V
SG_GROUPS = 8
SG_CHUNK = 128
SG_GROUP_DIM = D_MODEL // SG_GROUPS
REL_BUCKETS = 32
REL_MAX_DIST = 128
D_FF = 2816
EPS = 1e-6

LANES = 128
VMEM_LIMIT = 56 * 1024 * 1024

ROW_TILE = 512
ATTN_TQ = 512
ATTN_TK = 512
HG_BLOCK = 512
HG_CHUNK = 64
HG_LEVELS = 6
HG_TOT_ROWS = 16
HG_MAIN_ROWS = 2 * HG_CHUNK + HG_TOT_ROWS
HG_FAST_LOG2_LIMIT = -100.0
FFN_CHUNK = 2816

LOG2E = math.log2(math.e)
Q_SCALE = DA_QK_DIM ** -0.5 * LOG2E


def _cparams(sem):
    return pltpu.CompilerParams(dimension_semantics=sem, vmem_limit_bytes=VMEM_LIMIT)


def _resident(shape):
    nd = len(shape)
    return pl.BlockSpec(shape, lambda *_: (0,) * nd, pipeline_mode=pl.Buffered(1))


def _rms(x, gain):
    ms = jnp.mean(x * x, axis=-1, keepdims=True)
    return x * lax.rsqrt(ms + EPS) * gain


def _silu(x):
    return x * pl.reciprocal(1.0 + jnp.exp(-x), approx=True)


def _dot(a, b):
    return jnp.dot(a, b, preferred_element_type=F32)


def _dot_nt(a, b):
    return lax.dot_general(a, b, (((1,), (1,)), ((), ())), preferred_element_type=F32)


def _dot_tn(a, b):
    return lax.dot_general(a, b, (((0,), (0,)), ((), ())), preferred_element_type=F32)


def _band_kernel(rb_ref, o_ref):
    h = pl.program_id(0)
    shape = (LANES, 3 * LANES)
    qi = lax.broadcasted_iota(jnp.int32, shape, 0)
    xi = lax.broadcasted_iota(jnp.int32, shape, 1)
    rel = xi - LANES - qi
    half = REL_BUCKETS // 2
    max_exact = half // 2
    ret = jnp.where(rel > 0, half, 0)
    n = jnp.abs(rel)
    nf = jnp.maximum(n, 1).astype(F32)
    large = max_exact + (jnp.log(nf / max_exact) / math.log(REL_MAX_DIST / max_exact)
                         * (half - max_exact)).astype(jnp.int32)
    large = jnp.minimum(large, half - 1)
    bucket = ret + jnp.where(n < max_exact, n, large)
    out = jnp.zeros(shape, F32)
    for bkt in range(REL_BUCKETS):
        out = jnp.where(bucket == bkt, rb_ref[bkt * DA_HEADS + h], out)
    o_ref[...] = out


def _band_call(rel_bias):
    return pl.pallas_call(
        _band_kernel,
        grid=(DA_HEADS,),
        in_specs=[pl.BlockSpec(memory_space=pltpu.SMEM)],
        out_specs=pl.BlockSpec((None, LANES, 3 * LANES), lambda h: (h, 0, 0)),
        out_shape=jax.ShapeDtypeStruct((DA_HEADS, LANES, 3 * LANES), F32),
        name="rel_bias_band",
    )(rel_bias.reshape(-1))


def _even_in_kernel(x_ref, g_ref, wa_ref, wb_ref, oa_ref, ob_ref):
    hn = _rms(x_ref[...], g_ref[...]).astype(BF16)
    oa = _dot(hn, wa_ref[...])
    oa_ref[:, :DA_WIDTH] = (oa[:, :DA_WIDTH] * Q_SCALE).astype(BF16)
    oa_ref[:, DA_WIDTH:] = oa[:, DA_WIDTH:].astype(BF16)
    ob = _dot(hn, wb_ref[...])
    ob_ref[:, :HG_WIDTH] = _silu(ob[:, :HG_WIDTH])
    ob_ref[:, HG_WIDTH:] = ob[:, HG_WIDTH:]


def _even_in_call(h, gain, w_attn, w_hg):
    t = h.shape[0]
    tm = min(ROW_TILE, t)
    na, nb = w_attn.shape[1], w_hg.shape[1]
    return pl.pallas_call(
        _even_in_kernel,
        grid=(t // tm,),
        in_specs=[pl.BlockSpec((tm, D_MODEL), lambda i: (i, 0)),
                  _resident((1, D_MODEL)),
                  _resident((D_MODEL, na)),
                  _resident((D_MODEL, nb))],
        out_specs=[pl.BlockSpec((tm, na), lambda i: (i, 0)),
                   pl.BlockSpec((tm, nb), lambda i: (i, 0))],
        out_shape=[jax.ShapeDtypeStruct((t, na), BF16),
                   jax.ShapeDtypeStruct((t, nb), F32)],
        compiler_params=_cparams(("parallel",)),
        name="even_in_proj",
    )(h, gain, w_attn, w_hg)


def _attn_kernel(rb_ref, lamv_ref, q_ref, k_ref, v_ref, band_ref, sub_ref, o_ref, bias_scr,
                 *, tq, tk, seq, lambda_init):
    h = pl.program_id(0)
    qi = pl.program_id(1)
    b = pl.program_id(2)
    nrb = tq // LANES
    ncb = seq // LANES
    cpb = tk // LANES
    nkb = seq // tk

    @pl.when(b == 0)
    def _():
        far_l = rb_ref[(REL_BUCKETS // 2 - 1) * DA_HEADS + h]
        far_r = rb_ref[(REL_BUCKETS - 1) * DA_HEADS + h]
        for r in range(nrb):
            g = qi * nrb + r
            for c in range(ncb):
                d = jnp.zeros((LANES, LANES), jnp.int32) + (c - g)
                far = jnp.where(d < 0, far_l, far_r)
                tile = jnp.where(d == 0, band_ref[:, LANES:2 * LANES],
                                 jnp.where(d == -1, band_ref[:, 0:LANES],
                                           jnp.where(d == 1, band_ref[:, 2 * LANES:3 * LANES], far)))
                cc = c % cpb
                bias_scr[c // cpb, r * LANES:(r + 1) * LANES, cc * LANES:(cc + 1) * LANES] = (
                    tile * LOG2E)

    q = q_ref[...]
    lane = lax.broadcasted_iota(jnp.int32, q.shape, 1)
    zero = jnp.zeros_like(q)
    qq = jnp.concatenate([jnp.where(lane < DA_QK_DIM, q, zero),
                          jnp.where(lane < DA_QK_DIM, zero, q)], axis=0)

    def scores(kb):
        bt = bias_scr[kb]
        s = _dot_nt(qq, k_ref[kb * tk:(kb + 1) * tk, :]) + jnp.concatenate([bt, bt], axis=0)
        return s.astype(BF16)

    ones = jnp.ones((tk, LANES), BF16)
    m = jnp.full((2 * tq, 1), -jnp.inf, F32)
    acc = jnp.zeros((2 * tq, DA_V_DIM + LANES), F32)
    s = scores(0)
    for kb in range(nkb):
        s_next = scores(kb + 1) if kb + 1 < nkb else None
        m_new = jnp.maximum(m, jnp.max(s, axis=-1, keepdims=True).astype(F32))
        alpha = jnp.exp2(m - m_new)
        p = jnp.exp2(s - m_new.astype(BF16))
        v_ext = jnp.concatenate([v_ref[kb * tk:(kb + 1) * tk, :], ones], axis=1)
        acc = alpha * acc + _dot(p, v_ext)
        m = m_new
        s = s_next

    lamv = lamv_ref[...]
    lam = (jnp.exp(jnp.sum(lamv[0:1] * lamv[1:2], axis=-1, keepdims=True))
           - jnp.exp(jnp.sum(lamv[2:3] * lamv[3:4], axis=-1, keepdims=True)) + lambda_init)
    sm = acc[:, :DA_V_DIM] * pl.reciprocal(acc[:, DA_V_DIM:], approx=True)
    o = sm[:tq] - lam * sm[tq:]
    o_ref[...] = (_rms(o, sub_ref[...]) * (1.0 - lambda_init)).astype(BF16)


def _attn_call(qkv, band, rel_bias, lamv, subln, lambda_init):
    b, s, _ = qkv.shape
    tq = min(ATTN_TQ, s)
    tk = min(ATTN_TK, s)
    kern = functools.partial(_attn_kernel, tq=tq, tk=tk, seq=s, lambda_init=lambda_init)
    return pl.pallas_call(
        kern,
        grid=(DA_HEADS, s // tq, b),
        in_specs=[pl.BlockSpec(memory_space=pltpu.SMEM),
                  _resident((4, DA_QK_DIM)),
                  pl.BlockSpec((None, tq, DA_V_DIM), lambda h, i, bb: (bb, i, h)),
                  pl.BlockSpec((None, s, DA_V_DIM), lambda h, i, bb: (bb, 0, DA_HEADS + h)),
                  pl.BlockSpec((None, s, DA_V_DIM), lambda h, i, bb: (bb, 0, 2 * DA_HEADS + h)),
                  pl.BlockSpec((None, LANES, 3 * LANES), lambda h, i, bb: (h, 0, 0)),
                  _resident((1, DA_V_DIM))],
        out_specs=pl.BlockSpec((None, tq, DA_V_DIM), lambda h, i, bb: (bb, i, h)),
        out_shape=jax.ShapeDtypeStruct((b, s, DA_WIDTH), BF16),
        scratch_shapes=[pltpu.VMEM((s // tk, tq, tk), F32)],
        compiler_params=_cparams(("parallel", "parallel", "arbitrary")),
        name="diff_attention",
    )(rel_bias.reshape(-1), lamv, qkv, qkv, qkv, band, subln)


def _hg_sum_matrix(reverse):
    c = HG_CHUNK
    r = np.arange(c)
    blocks = []
    if not reverse:
        blocks.append(r[None, :] <= r[:, None])
        blocks.append(r[None, :] > r[:, None])
    else:
        blocks.append(r[None, :] >= r[:, None])
        blocks.append(r[None, :] < r[:, None])
    blocks.append(np.ones((HG_TOT_ROWS, c), bool))
    for lvl in range(HG_LEVELS):
        hs = 1 << lvl
        base = r & ~(2 * hs - 1)
        upper = ((r >> lvl) & 1) == 1
        m = np.zeros((c, c), bool)
        for i in range(c):
            if not reverse:
                bnd = base[i] + hs - 1
                lo, hi = (bnd + 1, i) if upper[i] else (i + 1, bnd)
            else:
                bnd = base[i] + hs
                lo, hi = (bnd, i - 1) if upper[i] else (i, bnd - 1)
            m[i, lo:hi + 1] = True
        blocks.append(m)
    m = np.concatenate(blocks, axis=0).astype(np.float32)
    return np.concatenate([m, m, m], axis=1)


def _hg_block(q_ref, z_ref, v_ref, lb_ref, m_ref, o_ref, st_ref, inter_ref, nsc, reverse):
    c = HG_CHUNK
    q = q_ref[...]
    z = z_ref[...]
    v = v_ref[...].astype(BF16)
    lb = lb_ref[...]

    z2 = z * LOG2E
    e = jnp.exp2(-jnp.abs(z2))
    r = pl.reciprocal(1.0 + e, approx=True)
    pos = z >= 0
    sig = jnp.where(pos, r, e * r)
    kk = (1.0 - lb) * jnp.where(pos, e * r, r)
    has_lb = lb > 0
    log_f = (jnp.where(has_lb, 0.0, jnp.minimum(z2, 0.0))
             + jnp.log2(jnp.where(has_lb, lb + (1.0 - lb) * sig, r)))

    def side_by_side(x):
        return jnp.concatenate([x[i * c:(i + 1) * c] for i in range(nsc)], axis=1)

    def chunk(x, i):
        return x[:, i * HG_K:(i + 1) * HG_K]

    l1 = log_f.astype(BF16)
    r1 = log_f - l1.astype(F32)
    l2 = r1.astype(BF16)
    l3 = (r1 - l2.astype(F32)).astype(BF16)
    lcat = jnp.concatenate([side_by_side(l1), side_by_side(l2), side_by_side(l3)], axis=0)
    sums = _dot(m_ref[0:HG_MAIN_ROWS, :], lcat)
    cum = sums[0:c]
    rest = sums[c:2 * c]
    tot = sums[2 * c:2 * c + 1]

    ql = side_by_side(q)
    kl = side_by_side(kk)
    qe = (ql * jnp.exp2(cum)).astype(BF16)
    kd = (kl * jnp.exp2(rest)).astype(BF16)
    st_decay = jnp.exp2(tot)

    ri = lax.broadcasted_iota(jnp.int32, (c, c), 0)
    ci = lax.broadcasted_iota(jnp.int32, (c, c), 1)

    mid = cum[c // 2:c // 2 + 1]
    qx = (ql * jnp.exp2(cum - mid)).astype(BF16)
    kx = (kl * jnp.exp2(mid - cum)).astype(BF16)
    causal = (ri <= ci) if reverse else (ri >= ci)
    o_intra = [_dot(jnp.where(causal, _dot_nt(chunk(qx, i), chunk(kx, i)), 0.0).astype(BF16),
                    v[i * c:(i + 1) * c]) for i in range(nsc)]

    st_add = [_dot_tn(v[i * c:(i + 1) * c], chunk(kd, i)) for i in range(nsc)]
    st = st_ref[...]
    for i in (reversed(range(nsc)) if reverse else range(nsc)):
        inter = _dot_nt(chunk(qe, i), st.astype(BF16))
        inter_ref[i * c:(i + 1) * c, :] = inter
        o_ref[i * c:(i + 1) * c, :] = o_intra[i] + inter
        st = st * chunk(st_decay, i) + st_add[i]
    st_ref[...] = st

    def redo_if_unsafe():
        @pl.when(jnp.min(tot) < HG_FAST_LOG2_LIMIT)
        def _():
            lsum = _dot(m_ref[HG_MAIN_ROWS:HG_MAIN_ROWS + HG_LEVELS * c, :], lcat)
            row = lax.broadcasted_iota(jnp.int32, ql.shape, 0)
            qb = ql.astype(BF16)
            kb = kl.astype(BF16)
            att = [jnp.where(ri == ci, _dot_nt(chunk(qb, i), chunk(kb, i)), 0.0)
                   for i in range(nsc)]
            for lvl in range(HG_LEVELS):
                dec = jnp.exp2(lsum[lvl * c:(lvl + 1) * c])
                later = ((row >> lvl) & 1) == 1
                q_rows = jnp.logical_not(later) if reverse else later
                x = jnp.where(q_rows, ql, kl) * dec
                qt = jnp.where(q_rows, x, 0.0).astype(BF16)
                kt = jnp.where(q_rows, 0.0, x).astype(BF16)
                same = (ri >> (lvl + 1)) == (ci >> (lvl + 1))
                att = [att[i] + jnp.where(same, _dot_nt(chunk(qt, i), chunk(kt, i)), 0.0)
                       for i in range(nsc)]
            for i in range(nsc):
                rows = slice(i * c, (i + 1) * c)
                o_ref[rows, :] = _dot(att[i].astype(BF16), v[rows]) + inter_ref[rows, :]

    return redo_if_unsafe


def _hgrn_kernel(qf_ref, zf_ref, vf_ref, qb_ref, zb_ref, vb_ref, lbf_ref, lbb_ref, mf_ref, mb_ref,
                 of_ref, ob_ref, sf_ref, sb_ref, if_ref, ib_ref, *, tb):
    @pl.when(pl.program_id(2) == 0)
    def _():
        sf_ref[...] = jnp.zeros_like(sf_ref)
        sb_ref[...] = jnp.zeros_like(sb_ref)

    nsc = tb // HG_CHUNK
    redo_f = _hg_block(qf_ref, zf_ref, vf_ref, lbf_ref, mf_ref, of_ref, sf_ref, if_ref, nsc, False)
    redo_b = _hg_block(qb_ref, zb_ref, vb_ref, lbb_ref, mb_ref, ob_ref, sb_ref, ib_ref, nsc, True)
    redo_f()
    redo_b()


def _hgrn_call(hg, lb_f, lb_b):
    b, s, _ = hg.shape
    tb = min(HG_BLOCK, s)
    nb = s // tb
    hh = HG_HEADS
    mf = jnp.asarray(_hg_sum_matrix(False), BF16)
    mb = jnp.asarray(_hg_sum_matrix(True), BF16)

    def fwd(col):
        return pl.BlockSpec((None, tb, HG_K), lambda bb, h, i: (bb, i, col * hh + h))

    def bwd(col):
        return pl.BlockSpec((None, tb, HG_K), lambda bb, h, i: (bb, nb - 1 - i, col * hh + h))

    lb_spec = pl.BlockSpec((None, 1, HG_K), lambda bb, h, i: (h, 0, 0))
    return pl.pallas_call(
        functools.partial(_hgrn_kernel, tb=tb),
        grid=(b, hh, nb),
        in_specs=[fwd(0), fwd(1), fwd(3), bwd(0), bwd(2), bwd(3), lb_spec, lb_spec,
                  _resident(mf.shape), _resident(mb.shape)],
        out_specs=[pl.BlockSpec((None, tb, HG_V), lambda bb, h, i: (bb, i, h)),
                   pl.BlockSpec((None, tb, HG_V), lambda bb, h, i: (bb, nb - 1 - i, h))],
        out_shape=[jax.ShapeDtypeStruct((b, s, HG_WIDTH), F32),
                   jax.ShapeDtypeStruct((b, s, HG_WIDTH), F32)],
        scratch_shapes=[pltpu.VMEM((HG_V, HG_K), F32), pltpu.VMEM((HG_V, HG_K), F32),
                        pltpu.VMEM((tb, HG_V), F32), pltpu.VMEM((tb, HG_V), F32)],
        compiler_params=_cparams(("parallel", "parallel", "arbitrary")),
        name="hgrn2_bidir",
    )(hg, hg, hg, hg, hg, hg, lb_f.reshape(hh, 1, HG_K), lb_b.reshape(hh, 1, HG_K), mf, mb)


def _even_tail_kernel(oa_ref, of_ref, ob_ref, gate_ref, h_ref, gn_ref, wo_ref,
                      g_ref, win_ref, wout_ref, fg_ref, o_ref, *, final):
    ob = of_ref[...] + ob_ref[...]
    parts = [oa_ref[...]]
    for hh in range(HG_HEADS):
        cols = slice(hh * HG_V, (hh + 1) * HG_V)
        y = _rms(ob[:, cols], gn_ref[...]) * _silu(gate_ref[:, cols])
        parts.append(y.astype(BF16))
    x = h_ref[...] + _dot(jnp.concatenate(parts, axis=-1), wo_ref[...])
    o_ref[...] = _ffn_body(x, g_ref, win_ref, wout_ref, fg_ref, final)


def _even_tail_call(o_a, o_f, o_b, hg, h, hg_norm, w_o, gain, w_in, w_out, final_gain, final):
    t = h.shape[0]
    tm = min(ROW_TILE, t)
    gate_blk = hg.shape[1] // HG_WIDTH - 1
    return pl.pallas_call(
        functools.partial(_even_tail_kernel, final=final),
        grid=(t // tm,),
        in_specs=[pl.BlockSpec((tm, DA_WIDTH), lambda i: (i, 0)),
                  pl.BlockSpec((tm, HG_WIDTH), lambda i: (i, 0)),
                  pl.BlockSpec((tm, HG_WIDTH), lambda i: (i, 0)),
                  pl.BlockSpec((tm, HG_WIDTH), lambda i: (i, gate_blk)),
                  pl.BlockSpec((tm, D_MODEL), lambda i: (i, 0)),
                  _resident((1, HG_V)),
                  _resident(w_o.shape)] + _ffn_specs(w_in, w_out),
        out_specs=pl.BlockSpec((tm, D_MODEL), lambda i: (i, 0)),
        out_shape=jax.ShapeDtypeStruct((t, D_MODEL), F32),
        compiler_params=_cparams(("parallel",)),
        name="even_out_ffn",
    )(o_a, o_f, o_b, hg, h, hg_norm, w_o, gain, w_in, w_out, final_gain)


def _odd_kernel(h_ref, g_ref, win_ref, sgn_ref, sgw_ref, sgb_ref, wout_ref, o_ref, vm_ref, *, tm):
    x = h_ref[...]
    hn = _rms(x, g_ref[...]).astype(BF16)
    y = _dot(hn, win_ref[...])
    y = 0.5 * y * (1.0 + lax.erf(y * math.sqrt(0.5)))
    u = y[:, :D_MODEL]
    v = _rms(y[:, D_MODEL:], sgn_ref[...]).astype(BF16)
    for c in range(tm // SG_CHUNK):
        rows = slice(c * SG_CHUNK, (c + 1) * SG_CHUNK)
        for g in range(SG_GROUPS):
            cols = slice(g * SG_GROUP_DIM, (g + 1) * SG_GROUP_DIM)
            vm_ref[rows, cols] = _dot(sgw_ref[g], v[rows, cols]) + sgb_ref[:, cols]
    t = (u * vm_ref[...]).astype(BF16)
    o_ref[...] = x + _dot(t, wout_ref[...])


def _odd_call(h, gain, w_in, sg_norm, sg_w, sg_b_full, w_out):
    t = h.shape[0]
    tm = min(ROW_TILE, t)
    return pl.pallas_call(
        functools.partial(_odd_kernel, tm=tm),
        grid=(t // tm,),
        in_specs=[pl.BlockSpec((tm, D_MODEL), lambda i: (i, 0)),
                  _resident((1, D_MODEL)),
                  _resident(w_in.shape),
                  _resident((1, D_MODEL)),
                  _resident(sg_w.shape),
                  _resident(sg_b_full.shape),
                  _resident(w_out.shape)],
        out_specs=pl.BlockSpec((tm, D_MODEL), lambda i: (i, 0)),
        out_shape=jax.ShapeDtypeStruct((t, D_MODEL), F32),
        scratch_shapes=[pltpu.VMEM((tm, D_MODEL), F32)],
        compiler_params=_cparams(("parallel",)),
        name="odd_mixer",
    )(h, gain, w_in, sg_norm, sg_w, sg_b_full, w_out)


def _ffn_body(x, g_ref, win_ref, wout_ref, fg_ref, final):
    hn = _rms(x, g_ref[...]).astype(BF16)
    acc = x
    for c in range(D_FF // FFN_CHUNK):
        lo = c * FFN_CHUNK
        gate = _dot(hn, win_ref[:, lo:lo + FFN_CHUNK])
        up = _dot(hn, win_ref[:, D_FF + lo:D_FF + lo + FFN_CHUNK])
        act = (_silu(gate) * up).astype(BF16)
        acc = acc + _dot(act, wout_ref[lo:lo + FFN_CHUNK, :])
    if final:
        acc = _rms(acc, fg_ref[...])
    return acc


def _ffn_kernel(h_ref, g_ref, win_ref, wout_ref, fg_ref, o_ref, *, final):
    o_ref[...] = _ffn_body(h_ref[...], g_ref, win_ref, wout_ref, fg_ref, final)


def _ffn_specs(w_in, w_out):
    return [_resident((1, D_MODEL)), _resident(w_in.shape), _resident(w_out.shape),
            _resident((1, D_MODEL))]


def _ffn_call(h, gain, w_in, w_out, final_gain, final):
    t = h.shape[0]
    tm = min(ROW_TILE, t)
    return pl.pallas_call(
        functools.partial(_ffn_kernel, final=final),
        grid=(t // tm,),
        in_specs=[pl.BlockSpec((tm, D_MODEL), lambda i: (i, 0))] + _ffn_specs(w_in, w_out),
        out_specs=pl.BlockSpec((tm, D_MODEL), lambda i: (i, 0)),
        out_shape=jax.ShapeDtypeStruct((t, D_MODEL), F32),
        compiler_params=_cparams(("parallel",)),
        name="swiglu_ffn",
    )(h, gain, w_in, w_out, final_gain)


def _lower_bounds(p):
    lb = jnp.cumsum(jax.nn.softmax(p.astype(F32), axis=0), axis=0)
    return lb - lb[:1]


def kernel(x, rel_bias, norm_mix, norm_ffn, norm_final, w_in_even, w_out_even,
           lambda_q1, lambda_k1, lambda_q2, lambda_k2, da_subln, hg_lb_fwd, hg_lb_bwd, hg_norm,
           w_in_odd, sg_norm, sg_w, sg_b, w_out_odd, w_ffn_in, w_ffn_out):
    b, s, d = x.shape
    t = b * s
    lb_f = _lower_bounds(hg_lb_fwd)
    lb_b = _lower_bounds(hg_lb_bwd)
    band = _band_call(rel_bias.astype(F32))
    n_attn = 3 * DA_WIDTH
    h = x.reshape(t, d)
    for l in range(DEPTH):
        gain = norm_mix[l].reshape(1, d)
        ffn = (norm_ffn[l].reshape(1, d), w_ffn_in[l].astype(BF16), w_ffn_out[l].astype(BF16),
               norm_final.reshape(1, d), l == DEPTH - 1)
        if l % 2 == 0:
            e = l // 2
            lambda_init = 0.8 - 0.6 * math.exp(-0.3 * l)
            w_in = w_in_even[e].astype(BF16)
            qkv, hg = _even_in_call(h, gain, w_in[:, :n_attn], w_in[:, n_attn:])
            lamv = jnp.stack([lambda_q1[e], lambda_k1[e], lambda_q2[e], lambda_k2[e]]).astype(F32)
            o_a = _attn_call(qkv.reshape(b, s, n_attn), band, rel_bias.astype(F32), lamv,
                             da_subln[e].reshape(1, DA_V_DIM), lambda_init)
            o_f, o_b = _hgrn_call(hg.reshape(b, s, -1), lb_f[e], lb_b[e])
            h = _even_tail_call(o_a.reshape(t, DA_WIDTH), o_f.reshape(t, HG_WIDTH),
                                o_b.reshape(t, HG_WIDTH), hg, h,
                                hg_norm[e].reshape(1, HG_V), w_out_even[e].astype(BF16), *ffn)
        else:
            o = l // 2
            sg_b_full = jnp.repeat(sg_b[o].T.astype(F32), SG_GROUP_DIM, axis=1)
            h = _odd_call(h, gain, w_in_odd[o].astype(BF16), sg_norm[o].reshape(1, d),
                          sg_w[o].astype(BF16), sg_b_full, w_out_odd[o].astype(BF16))
            h = _ffn_call(h, *ffn)
    return h.reshape(b, s, d)
```

```python
import functools
import itertools
import math

import numpy as np
import jax
import jax.numpy as jnp
from jax import lax
from jax.experimental import pallas as pl
from jax.experimental.pallas import tpu as pltpu

F32 = jnp.float32
BF16 = jnp.bfloat16

D_MODEL = 1024
DEPTH = 4
DA_HEADS = 4
DA_QK_DIM = 64
DA_V_DIM = 128
DA_WIDTH = DA_HEADS * DA_V_DIM
HG_HEADS = 4
HG_K = 128
HG_V = 128
HG_WIDTH = HG_HEADS * HG_V
SG_GROUPS = 8
SG_CHUNK = 128
SG_GROUP_DIM = D_MODEL // SG_GROUPS
REL_BUCKETS = 32
REL_MAX_DIST = 128
D_FF = 2816
EPS = 1e-6

LANES = 128
VMEM_LIMIT = 56 * 1024 * 1024

ROW_TILE = 512
CORE_BLOCK = 512
HG_CHUNK = 64
HG_LEVELS = 6
HG_TOT_ROWS = 16
HG_MAIN_ROWS = 2 * HG_CHUNK + HG_TOT_ROWS
HG_FAST_LOG2_LIMIT = -100.0
FFN_CHUNK = 2816

LOG2E = math.log2(math.e)
Q_SCALE = DA_QK_DIM ** -0.5 * LOG2E


def _cparams(sem):
    return pltpu.CompilerParams(dimension_semantics=sem, vmem_limit_bytes=VMEM_LIMIT)


def _resident(shape):
    nd = len(shape)
    return pl.BlockSpec(shape, lambda *_: (0,) * nd, pipeline_mode=pl.Buffered(1))


def _rms(x, gain):
    ms = jnp.mean(x * x, axis=-1, keepdims=True)
    return x * lax.rsqrt(ms + EPS) * gain


def _silu(x):
    return x * pl.reciprocal(1.0 + jnp.exp(-x), approx=True)


def _dot(a, b):
    return jnp.dot(a, b, preferred_element_type=F32)


def _dot_nt(a, b):
    return lax.dot_general(a, b, (((1,), (1,)), ((), ())), preferred_element_type=F32)


def _dot_tn(a, b):
    return lax.dot_general(a, b, (((0,), (0,)), ((), ())), preferred_element_type=F32)


def _band_kernel(rb_ref, o_ref):
    h = pl.program_id(0)
    shape = (LANES, 3 * LANES)
    qi = lax.broadcasted_iota(jnp.int32, shape, 0)
    xi = lax.broadcasted_iota(jnp.int32, shape, 1)
    rel = xi - LANES - qi
    half = REL_BUCKETS // 2
    max_exact = half // 2
    ret = jnp.where(rel > 0, half, 0)
    n = jnp.abs(rel)
    nf = jnp.maximum(n, 1).astype(F32)
    large = max_exact + (jnp.log(nf / max_exact) / math.log(REL_MAX_DIST / max_exact)
                         * (half - max_exact)).astype(jnp.int32)
    large = jnp.minimum(large, half - 1)
    bucket = ret + jnp.where(n < max_exact, n, large)
    out = jnp.zeros(shape, F32)
    for bkt in range(REL_BUCKETS):
        out = jnp.where(bucket == bkt, rb_ref[bkt * DA_HEADS + h], out)
    o_ref[...] = out


def _band_call(rel_bias):
    return pl.pallas_call(
        _band_kernel,
        grid=(DA_HEADS,),
        in_specs=[pl.BlockSpec(memory_space=pltpu.SMEM)],
        out_specs=pl.BlockSpec((None, LANES, 3 * LANES), lambda h: (h, 0, 0)),
        out_shape=jax.ShapeDtypeStruct((DA_HEADS, LANES, 3 * LANES), F32),
        name="rel_bias_band",
    )(rel_bias.reshape(-1))


def _even_in_kernel(x_ref, g_ref, wa_ref, wb_ref, oa_ref, ob_ref):
    hn = _rms(x_ref[...], g_ref[...]).astype(BF16)
    oa = _dot(hn, wa_ref[...])
    oa_ref[:, :DA_WIDTH] = (oa[:, :DA_WIDTH] * Q_SCALE).astype(BF16)
    oa_ref[:, DA_WIDTH:] = oa[:, DA_WIDTH:].astype(BF16)
    ob = _dot(hn, wb_ref[...])
    ob_ref[:, :HG_WIDTH] = _silu(ob[:, :HG_WIDTH])
    ob_ref[:, HG_WIDTH:] = ob[:, HG_WIDTH:]


def _even_in_call(h, gain, w_attn, w_hg):
    t = h.shape[0]
    tm = min(ROW_TILE, t)
    na, nb = w_attn.shape[1], w_hg.shape[1]
    return pl.pallas_call(
        _even_in_kernel,
        grid=(t // tm,),
        in_specs=[pl.BlockSpec((tm, D_MODEL), lambda i: (i, 0)),
                  _resident((1, D_MODEL)),
                  _resident((D_MODEL, na)),
                  _resident((D_MODEL, nb))],
        out_specs=[pl.BlockSpec((tm, na), lambda i: (i, 0)),
                   pl.BlockSpec((tm, nb), lambda i: (i, 0))],
        out_shape=[jax.ShapeDtypeStruct((t, na), BF16),
                   jax.ShapeDtypeStruct((t, nb), F32)],
        compiler_params=_cparams(("parallel",)),
        name="even_in_proj",
    )(h, gain, w_attn, w_hg)


def _build_bias(rb_ref, band_ref, bias_scr, h, tq):
    far_l = rb_ref[(REL_BUCKETS // 2 - 1) * DA_HEADS + h]
    far_r = rb_ref[(REL_BUCKETS - 1) * DA_HEADS + h]
    nt = tq // LANES
    for dd in range(5):
        for r in range(nt):
            for c in range(nt):
                d = nt * (dd - 2) + c - r
                if abs(d) <= 1:
                    tile = band_ref[:, (d + 1) * LANES:(d + 2) * LANES]
                else:
                    tile = jnp.full((LANES, LANES), far_l if d < 0 else far_r, F32)
                bias_scr[dd, r * LANES:(r + 1) * LANES, c * LANES:(c + 1) * LANES] = tile * LOG2E


def _attn_steps(lamv_ref, q_ref, k_ref, v_ref, sub_ref, o_ref, bias_scr, qi,
                *, tq, seq, lambda_init):
    tk = tq
    nkb = seq // tk
    q = q_ref[...]
    lane = lax.broadcasted_iota(jnp.int32, q.shape, 1)
    zero = jnp.zeros_like(q)
    qq = jnp.concatenate([jnp.where(lane < DA_QK_DIM, q, zero),
                          jnp.where(lane < DA_QK_DIM, zero, q)], axis=0)

    def scores(kb):
        bt = bias_scr[jnp.clip(kb - qi, -2, 2) + 2]
        s = _dot_nt(qq, k_ref[kb * tk:(kb + 1) * tk, :]) + jnp.concatenate([bt, bt], axis=0)
        return s.astype(BF16)

    ones = jnp.ones((tk, LANES), BF16)
    m = jnp.full((2 * tq, 1), -jnp.inf, F32)
    acc = jnp.zeros((2 * tq, DA_V_DIM + LANES), F32)
    s = scores(0)
    yield
    for kb in range(nkb):
        s_next = scores(kb + 1) if kb + 1 < nkb else None
        m_new = jnp.maximum(m, jnp.max(s, axis=-1, keepdims=True).astype(F32))
        alpha = jnp.exp2(m - m_new)
        p = jnp.exp2(s - m_new.astype(BF16))
        v_ext = jnp.concatenate([v_ref[kb * tk:(kb + 1) * tk, :], ones], axis=1)
        acc = alpha * acc + _dot(p, v_ext)
        m = m_new
        s = s_next
        yield

    lamv = lamv_ref[...]
    lam = (jnp.exp(jnp.sum(lamv[0:1] * lamv[1:2], axis=-1, keepdims=True))
           - jnp.exp(jnp.sum(lamv[2:3] * lamv[3:4], axis=-1, keepdims=True)) + lambda_init)
    sm = acc[:, :DA_V_DIM] * pl.reciprocal(acc[:, DA_V_DIM:], approx=True)
    o = sm[:tq] - lam * sm[tq:]
    o_ref[...] = (_rms(o, sub_ref[...]) * (1.0 - lambda_init)).astype(BF16)


def _hg_sum_matrix(reverse):
    c = HG_CHUNK
    r = np.arange(c)
    blocks = []
    if not reverse:
        blocks.append(r[None, :] <= r[:, None])
        blocks.append(r[None, :] > r[:, None])
    else:
        blocks.append(r[None, :] >= r[:, None])
        blocks.append(r[None, :] < r[:, None])
    blocks.append(np.ones((HG_TOT_ROWS, c), bool))
    for lvl in range(HG_LEVELS):
        hs = 1 << lvl
        base = r & ~(2 * hs - 1)
        upper = ((r >> lvl) & 1) == 1
        m = np.zeros((c, c), bool)
        for i in range(c):
            if not reverse:
                bnd = base[i] + hs - 1
                lo, hi = (bnd + 1, i) if upper[i] else (i + 1, bnd)
            else:
                bnd = base[i] + hs
                lo, hi = (bnd, i - 1) if upper[i] else (i, bnd - 1)
            m[i, lo:hi + 1] = True
        blocks.append(m)
    m = np.concatenate(blocks, axis=0).astype(np.float32)
    return np.concatenate([m, m, m], axis=1)


def _hg_steps(q_ref, z_ref, v_ref, lb_ref, m_ref, o_ref, st_ref, inter_ref, nsc, reverse, redo):
    c = HG_CHUNK
    q = q_ref[...]
    z = z_ref[...]
    v = v_ref[...].astype(BF16)
    lb = lb_ref[...]

    z2 = z * LOG2E
    e = jnp.exp2(-jnp.abs(z2))
    r = pl.reciprocal(1.0 + e, approx=True)
    pos = z >= 0
    sig = jnp.where(pos, r, e * r)
    kk = (1.0 - lb) * jnp.where(pos, e * r, r)
    has_lb = lb > 0
    log_f = (jnp.where(has_lb, 0.0, jnp.minimum(z2, 0.0))
             + jnp.log2(jnp.where(has_lb, lb + (1.0 - lb) * sig, r)))
    yield

    def side_by_side(x):
        return jnp.concatenate([x[i * c:(i + 1) * c] for i in range(nsc)], axis=1)

    def chunk(x, i):
        return x[:, i * HG_K:(i + 1) * HG_K]

    l1 = log_f.astype(BF16)
    r1 = log_f - l1.astype(F32)
    l2 = r1.astype(BF16)
    l3 = (r1 - l2.astype(F32)).astype(BF16)
    lcat = jnp.concatenate([side_by_side(l1), side_by_side(l2), side_by_side(l3)], axis=0)
    sums = _dot(m_ref[0:HG_MAIN_ROWS, :], lcat)
    cum = sums[0:c]
    rest = sums[c:2 * c]
    tot = sums[2 * c:2 * c + 1]
    yield

    ql = side_by_side(q)
    kl = side_by_side(kk)
    qe = (ql * jnp.exp2(cum)).astype(BF16)
    kd = (kl * jnp.exp2(rest)).astype(BF16)
    st_decay = jnp.exp2(tot)

    ri = lax.broadcasted_iota(jnp.int32, (c, c), 0)
    ci = lax.broadcasted_iota(jnp.int32, (c, c), 1)

    mid = cum[c // 2:c // 2 + 1]
    qx = (ql * jnp.exp2(cum - mid)).astype(BF16)
    kx = (kl * jnp.exp2(mid - cum)).astype(BF16)
    yield
    causal = (ri <= ci) if reverse else (ri >= ci)
    o_intra = [_dot(jnp.where(causal, _dot_nt(chunk(qx, i), chunk(kx, i)), 0.0).astype(BF16),
                    v[i * c:(i + 1) * c]) for i in range(nsc)]
    st_add =[_dot_tn(v[i * c:(i + 1) * c], chunk(kd, i)) for i in range(nsc)]
    yield
    st = st_ref[...]
    for i in (reversed(range(nsc)) if reverse else range(nsc)):
        inter = _dot_nt(chunk(qe, i), st.astype(BF16))
        inter_ref[i * c:(i + 1) * c, :] = inter
        o_ref[i * c:(i + 1) * c, :] = o_intra[i] + inter
        st = st * chunk(st_decay, i) + st_add[i]
    st_ref[...] = st

    def redo_if_unsafe():
        @pl.when(jnp.min(tot) < HG_FAST_LOG2_LIMIT)
        def _():
            lsum = _dot(m_ref[HG_MAIN_ROWS:HG_MAIN_ROWS + HG_LEVELS * c, :], lcat)
            row = lax.broadcasted_iota(jnp.int32, ql.shape, 0)
            qb = ql.astype(BF16)
            kb = kl.astype(BF16)
            att = [jnp.where(ri == ci, _dot_nt(chunk(qb, i), chunk(kb, i)), 0.0)
                   for i in range(nsc)]
            for lvl in range(HG_LEVELS):
                dec = jnp.exp2(lsum[lvl * c:(lvl + 1) * c])
                later = ((row >> lvl) & 1) == 1
                q_rows = jnp.logical_not(later) if reverse else later
                x = jnp.where(q_rows, ql, kl) * dec
                qt = jnp.where(q_rows, x, 0.0).astype(BF16)
                kt = jnp.where(q_rows, 0.0, x).astype(BF16)
                same = (ri >> (lvl + 1)) == (ci >> (lvl + 1))
                att = [att[i] + jnp.where(same, _dot_nt(chunk(qt, i), chunk(kt, i)), 0.0)
                       for i in range(nsc)]
            for i in range(nsc):
                rows = slice(i * c, (i + 1) * c)
                o_ref[rows, :] = _dot(att[i].astype(BF16), v[rows]) + inter_ref[rows, :]

    redo.append(redo_if_unsafe)


def _even_core_kernel(rb_ref, lamv_ref, q_ref, k_ref, v_ref, band_ref, sub_ref,
                      qf_ref, zf_ref, vf_ref, qb_ref, zb_ref, vb_ref, lbf_ref, lbb_ref,
                      mf_ref, mb_ref, oa_ref, of_ref, ob_ref,
                      bias_scr, sf_ref, sb_ref, if_ref, ib_ref, *, tb, seq, lambda_init):
    h = pl.program_id(0)
    b = pl.program_id(1)
    blk = pl.program_id(2)

    @pl.when(jnp.logical_and(b == 0, blk == 0))
    def _():
        _build_bias(rb_ref, band_ref, bias_scr, h, tb)

    @pl.when(blk == 0)
    def _():
        sf_ref[...] = jnp.zeros_like(sf_ref)
        sb_ref[...] = jnp.zeros_like(sb_ref)

    nsc = tb // HG_CHUNK
    redo = []
    attn = _attn_steps(lamv_ref, q_ref, k_ref, v_ref, sub_ref, oa_ref, bias_scr, blk,
                       tq=tb, seq=seq, lambda_init=lambda_init)
    hgrn = [_hg_steps(qf_ref, zf_ref, vf_ref, lbf_ref, mf_ref, of_ref, sf_ref, if_ref,
                      nsc, False, redo),
            _hg_steps(qb_ref, zb_ref, vb_ref, lbb_ref, mb_ref, ob_ref, sb_ref, ib_ref,
                      nsc, True, redo)]
    hg_chain = itertools.chain(*hgrn)
    for _ in attn:
        next(hg_chain, None)
    for _ in hg_chain:
        pass
    for fn in redo:
        fn()


def _even_core_call(qkv, hg, band, rel_bias, lamv, subln, lb_f, lb_b, lambda_init):
    b, s, _ = qkv.shape
    tb = min(CORE_BLOCK, s)
    nb = s // tb
    hh = HG_HEADS
    assert DA_HEADS == HG_HEADS
    mf = jnp.asarray(_hg_sum_matrix(False), BF16)
    mb = jnp.asarray(_hg_sum_matrix(True), BF16)

    def fwd(col):
        return pl.BlockSpec((None, tb, HG_K), lambda h, bb, i: (bb, i, col * hh + h))

    def bwd(col):
        return pl.BlockSpec((None, tb, HG_K), lambda h, bb, i: (bb, nb - 1 - i, col * hh + h))

    lb_spec = pl.BlockSpec((None, 1, HG_K), lambda h, bb, i: (h, 0, 0))
    kern = functools.partial(_even_core_kernel, tb=tb, seq=s, lambda_init=lambda_init)
    return pl.pallas_call(
        kern,
        grid=(hh, b, nb),
        in_specs=[pl.BlockSpec(memory_space=pltpu.SMEM),
                  _resident((4, DA_QK_DIM)),
                  pl.BlockSpec((None, tb, DA_V_DIM), lambda h, bb, i: (bb, i, h)),
                  pl.BlockSpec((None, s, DA_V_DIM), lambda h, bb, i: (bb, 0, hh + h)),
                  pl.BlockSpec((None, s, DA_V_DIM), lambda h, bb, i: (bb, 0, 2 * hh + h)),
                  pl.BlockSpec((None, LANES, 3 * LANES), lambda h, bb, i: (h, 0, 0)),
                  _resident((1, DA_V_DIM)),
                  fwd(0), fwd(1), fwd(3), bwd(0), bwd(2), bwd(3), lb_spec, lb_spec,
                  _resident(mf.shape), _resident(mb.shape)],
        out_specs=[pl.BlockSpec((None, tb, DA_V_DIM), lambda h, bb, i: (bb, i, h)),
                   pl.BlockSpec((None, tb, HG_V), lambda h, bb, i: (bb, i, h)),
                   pl.BlockSpec((None, tb, HG_V), lambda h, bb, i: (bb, nb - 1 - i, h))],
        out_shape=[jax.ShapeDtypeStruct((b, s, DA_WIDTH), BF16),
                   jax.ShapeDtypeStruct((b, s, HG_WIDTH), F32),
                   jax.ShapeDtypeStruct((b, s, HG_WIDTH), F32)],
        scratch_shapes=[pltpu.VMEM((5, tb, tb), F32),
                        pltpu.VMEM((HG_V, HG_K), F32), pltpu.VMEM((HG_V, HG_K), F32),
                        pltpu.VMEM((tb, HG_V), F32), pltpu.VMEM((tb, HG_V), F32)],
        compiler_params=_cparams(("arbitrary", "arbitrary", "arbitrary")),
        name="even_core",
    )(rel_bias.reshape(-1), lamv, qkv, qkv, qkv, band, subln,
      hg, hg, hg, hg, hg, hg, lb_f.reshape(hh, 1, HG_K), lb_b.reshape(hh, 1, HG_K), mf, mb)


def _even_tail_kernel(oa_ref, of_ref, ob_ref, gate_ref, h_ref, gn_ref, wo_ref,
                      g_ref, win_ref, wout_ref, fg_ref, o_ref, *, final):
    ob = of_ref[...] + ob_ref[...]
    parts = [oa_ref[...]]
    for hh in range(HG_HEADS):
        cols = slice(hh * HG_V, (hh + 1) * HG_V)
        y = _rms(ob[:, cols], gn_ref[...]) * _silu(gate_ref[:, cols])
        parts.append(y.astype(BF16))
    x = h_ref[...] + _dot(jnp.concatenate(parts, axis=-1), wo_ref[...])
    o_ref[...] = _ffn_body(x, g_ref, win_ref, wout_ref, fg_ref, final)


def _even_tail_call(o_a, o_f, o_b, hg, h, hg_norm, w_o, gain, w_in, w_out, final_gain, final):
    t = h.shape[0]
    tm = min(ROW_TILE, t)
    gate_blk = hg.shape[1] // HG_WIDTH - 1
    return pl.pallas_call(
        functools.partial(_even_tail_kernel, final=final),
        grid=(t // tm,),
        in_specs=[pl.BlockSpec((tm, DA_WIDTH), lambda i: (i, 0)),
                  pl.BlockSpec((tm, HG_WIDTH), lambda i: (i, 0)),
                  pl.BlockSpec((tm, HG_WIDTH), lambda i: (i, 0)),
                  pl.BlockSpec((tm, HG_WIDTH), lambda i: (i, gate_blk)),
                  pl.BlockSpec((tm, D_MODEL), lambda i: (i, 0)),
                  _resident((1, HG_V)),
                  _resident(w_o.shape)] + _ffn_specs(w_in, w_out),
        out_specs=pl.BlockSpec((tm, D_MODEL), lambda i: (i, 0)),
        out_shape=jax.ShapeDtypeStruct((t, D_MODEL), F32),
        compiler_params=_cparams(("parallel",)),
        name="even_out_ffn",
    )(o_a, o_f, o_b, hg, h, hg_norm, w_o, gain, w_in, w_out, final_gain)


def _odd_kernel(h_ref, g_ref, win_ref, sgn_ref, sgw_ref, sgb_ref, wout_ref, o_ref, vm_ref, *, tm):
    x = h_ref[...]
    hn = _rms(x, g_ref[...]).astype(BF16)
    y = _dot(hn, win_ref[...])
    y = 0.5 * y * (1.0 + lax.erf(y * math.sqrt(0.5)))
    u = y[:, :D_MODEL]
    v = _rms(y[:, D_MODEL:], sgn_ref[...]).astype(BF16)
    for c in range(tm // SG_CHUNK):
        rows = slice(c * SG_CHUNK, (c + 1) * SG_CHUNK)
        for g in range(SG_GROUPS):
            cols = slice(g * SG_GROUP_DIM, (g + 1) * SG_GROUP_DIM)
            vm_ref[rows, cols] = _dot(sgw_ref[g], v[rows, cols]) + sgb_ref[:, cols]
    t = (u * vm_ref[...]).astype(BF16)
    o_ref[...] = x + _dot(t, wout_ref[...])


def _odd_call(h, gain, w_in, sg_norm, sg_w, sg_b_full, w_out):
    t = h.shape[0]
    tm = min(ROW_TILE, t)
    return pl.pallas_call(
        functools.partial(_odd_kernel, tm=tm),
        grid=(t // tm,),
        in_specs=[pl.BlockSpec((tm, D_MODEL), lambda i: (i, 0)),
                  _resident((1, D_MODEL)),
                  _resident(w_in.shape),
                  _resident((1, D_MODEL)),
                  _resident(sg_w.shape),
                  _resident(sg_b_full.shape),
                  _resident(w_out.shape)],
        out_specs=pl.BlockSpec((tm, D_MODEL), lambda i: (i, 0)),
        out_shape=jax.ShapeDtypeStruct((t, D_MODEL), F32),
        scratch_shapes=[pltpu.VMEM((tm, D_MODEL), F32)],
        compiler_params=_cparams(("parallel",)),
        name="odd_mixer",
    )(h, gain, w_in, sg_norm, sg_w, sg_b_full, w_out)


def _ffn_body(x, g_ref, win_ref, wout_ref, fg_ref, final):
    hn = _rms(x, g_ref[...]).astype(BF16)
    acc = x
    for c in range(D_FF // FFN_CHUNK):
        lo = c * FFN_CHUNK
        gate = _dot(hn, win_ref[:, lo:lo + FFN_CHUNK])
        up = _dot(hn, win_ref[:, D_FF + lo:D_FF + lo + FFN_CHUNK])
        act = (_silu(gate) * up).astype(BF16)
        acc = acc + _dot(act, wout_ref[lo:lo + FFN_CHUNK, :])
    if final:
        acc = _rms(acc, fg_ref[...])
    return acc


def _ffn_kernel(h_ref, g_ref, win_ref, wout_ref, fg_ref, o_ref, *, final):
    o_ref[...] = _ffn_body(h_ref[...], g_ref, win_ref, wout_ref, fg_ref, final)


def _ffn_specs(w_in, w_out):
    return [_resident((1, D_MODEL)), _resident(w_in.shape), _resident(w_out.shape),
            _resident((1, D_MODEL))]


def _ffn_call(h, gain, w_in, w_out, final_gain, final):
    t = h.shape[0]
    tm = min(ROW_TILE, t)
    return pl.pallas_call(
        functools.partial(_ffn_kernel, final=final),
        grid=(t // tm,),
        in_specs=[pl.BlockSpec((tm, D_MODEL), lambda i: (i, 0))] + _ffn_specs(w_in, w_out),
        out_specs=pl.BlockSpec((tm, D_MODEL), lambda i: (i, 0)),
        out_shape=jax.ShapeDtypeStruct((t, D_MODEL), F32),
        compiler_params=_cparams(("parallel",)),
        name="swiglu_ffn",
    )(h, gain, w_in, w_out, final_gain)


def _lower_bounds(p):
    lb = jnp.cumsum(jax.nn.softmax(p.astype(F32), axis=0), axis=0)
    return lb - lb[:1]


def kernel(x, rel_bias, norm_mix, norm_ffn, norm_final, w_in_even, w_out_even,
           lambda_q1, lambda_k1, lambda_q2, lambda_k2, da_subln, hg_lb_fwd, hg_lb_bwd, hg_norm,
           w_in_odd, sg_norm, sg_w, sg_b, w_out_odd, w_ffn_in, w_ffn_out):
    b, s, d = x.shape
    t = b * s
    lb_f = _lower_bounds(hg_lb_fwd)
    lb_b = _lower_bounds(hg_lb_bwd)
    band = _band_call(rel_bias.astype(F32))
    n_attn = 3 * DA_WIDTH
    h = x.reshape(t, d)
    for l in range(DEPTH):
        gain = norm_mix[l].reshape(1, d)
        ffn = (norm_ffn[l].reshape(1, d), w_ffn_in[l].astype(BF16), w_ffn_out[l].astype(BF16),
               norm_final.reshape(1, d), l == DEPTH - 1)
        if l % 2 == 0:
            e = l // 2
            lambda_init = 0.8 - 0.6 * math.exp(-0.3 * l)
            w_in = w_in_even[e].astype(BF16)
            qkv, hg = _even_in_call(h, gain, w_in[:, :n_attn], w_in[:, n_attn:])
            lamv = jnp.stack([lambda_q1[e], lambda_k1[e], lambda_q2[e], lambda_k2[e]]).astype(F32)
            o_a, o_f, o_b = _even_core_call(
                qkv.reshape(b, s, n_attn), hg.reshape(b, s, -1), band, rel_bias.astype(F32), lamv,
                da_subln[e].reshape(1, DA_V_DIM), lb_f[e], lb_b[e], lambda_init)
            h = _even_tail_call(o_a.reshape(t, DA_WIDTH), o_f.reshape(t, HG_WIDTH),
                                o_b.reshape(t, HG_WIDTH), hg, h,
                                hg_norm[e].reshape(1, HG_V), w_out_even[e].astype(BF16), *ffn)
        else:
            o = l // 2
            sg_b_full = jnp.repeat(sg_b[o].T.astype(F32), SG_GROUP_DIM, axis=1)
            h = _odd_call(h, gain, w_in_odd[o].astype(BF16), sg_norm[o].reshape(1, d),
                          sg_w[o].astype(BF16), sg_b_full, w_out_odd[o].astype(BF16))
            h = _ffn_call(h, *ffn)
    return h.reshape(b, s, d)
```

```python
import functools
import itertools
import math

import numpy as np
import jax
import jax.numpy as jnp
from jax import lax
from jax.experimental import pallas as pl
from jax.experimental.pallas import tpu as pltpu

F32 = jnp.float32
BF16 = jnp.bfloat16

D_MODEL = 1024
DEPTH = 4
DA_HEADS = 4
DA_QK_DIM = 64
DA_V_DIM = 128
DA_WIDTH = DA_HEADS * DA_V_DIM
HG_HEADS = 4
HG_K = 128
HG_V = 128
HG_WIDTH = HG_HEADS * HG_V
SG_GROUPS = 8
SG_CHUNK = 128
SG_GROUP_DIM = D_MODEL // SG_GROUPS
REL_BUCKETS = 32
REL_MAX_DIST = 128
D_FF = 2816
EPS = 1e-6

LANES = 128
VMEM_LIMIT = 56 * 1024 * 1024

ROW_TILE = 512
CORE_BLOCK = 512
HG_CHUNK = 64
HG_LEVELS = 6
HG_TOT_ROWS = 16
HG_MAIN_ROWS = 2 * HG_CHUNK + HG_TOT_ROWS
HG_FAST_LOG2_LIMIT = -100.0

LOG2E = math.log2(math.e)
Q_SCALE = DA_QK_DIM ** -0.5 * LOG2E


def _cparams(sem):
    return pltpu.CompilerParams(dimension_semantics=sem, vmem_limit_bytes=VMEM_LIMIT)


def _resident(shape):
    nd = len(shape)
    return pl.BlockSpec(shape, lambda *_: (0,) * nd, pipeline_mode=pl.Buffered(1))


def _rms(x, gain):
    ms = jnp.mean(x * x, axis=-1, keepdims=True)
    return x * lax.rsqrt(ms + EPS) * gain


def _silu(x):
    return x * pl.reciprocal(1.0 + jnp.exp(-x), approx=True)


def _dot(a, b):
    return jnp.dot(a, b, preferred_element_type=F32)


def _dot_nt(a, b):
    return lax.dot_general(a, b, (((1,), (1,)), ((), ())), preferred_element_type=F32)


def _dot_tn(a, b):
    return lax.dot_general(a, b, (((0,), (0,)), ((), ())), preferred_element_type=F32)


def _interleave(gens):
    live = []
    pending = list(gens)
    done = object()
    while live or pending:
        if pending:
            live.append(pending.pop(0))
        live = [g for g in live if next(g, done) is not done]


def _half_tiles(tm):
    nr = tm // 2 if tm % 16 == 0 else tm
    return [slice(r0, r0 + nr) for r0 in range(0, tm, nr)]


def _band_kernel(rb_ref, o_ref):
    h = pl.program_id(0)
    shape = (LANES, 3 * LANES)
    qi = lax.broadcasted_iota(jnp.int32, shape, 0)
    xi = lax.broadcasted_iota(jnp.int32, shape, 1)
    rel = xi - LANES - qi
    half = REL_BUCKETS // 2
    max_exact = half // 2
    ret = jnp.where(rel > 0, half, 0)
    n = jnp.abs(rel)
    nf = jnp.maximum(n, 1).astype(F32)
    large = max_exact + (jnp.log(nf / max_exact) / math.log(REL_MAX_DIST / max_exact)
                         * (half - max_exact)).astype(jnp.int32)
    large = jnp.minimum(large, half - 1)
    bucket = ret + jnp.where(n < max_exact, n, large)
    out = jnp.zeros(shape, F32)
    for bkt in range(REL_BUCKETS):
        out = jnp.where(bucket == bkt, rb_ref[bkt * DA_HEADS + h], out)
    o_ref[...] = out


def _band_call(rel_bias):
    return pl.pallas_call(
        _band_kernel,
        grid=(DA_HEADS,),
        in_specs=[pl.BlockSpec(memory_space=pltpu.SMEM)],
        out_specs=pl.BlockSpec((None, LANES, 3 * LANES), lambda h: (h, 0, 0)),
        out_shape=jax.ShapeDtypeStruct((DA_HEADS, LANES, 3 * LANES), F32),
        name="rel_bias_band",
    )(rel_bias.reshape(-1))


def _even_in_steps(x_ref, g_ref, w_ref, oa_ref, ob_ref, rows):
    na = oa_ref.shape[1]
    hn = _rms(x_ref[rows, :], g_ref[...]).astype(BF16)
    oa = _dot(hn, w_ref[:, :na])
    ob = _dot(hn, w_ref[:, na:])
    yield
    oa_ref[rows, :DA_WIDTH] = (oa[:, :DA_WIDTH] * Q_SCALE).astype(BF16)
    oa_ref[rows, DA_WIDTH:] = oa[:, DA_WIDTH:].astype(BF16)
    ob_ref[rows, :HG_WIDTH] = _silu(ob[:, :HG_WIDTH])
    ob_ref[rows, HG_WIDTH:] = ob[:, HG_WIDTH:]


def _even_in_kernel(x_ref, g_ref, w_ref, oa_ref, ob_ref):
    _interleave([_even_in_steps(x_ref, g_ref, w_ref, oa_ref, ob_ref, rows)
                 for rows in _half_tiles(x_ref.shape[0])])


def _even_in_call(h, gain, w_in, na):
    t = h.shape[0]
    tm = min(ROW_TILE, t)
    nb = w_in.shape[1] - na
    return pl.pallas_call(
        _even_in_kernel,
        grid=(t // tm,),
        in_specs=[pl.BlockSpec((tm, D_MODEL), lambda i: (i, 0)),
                  _resident((1, D_MODEL)),
                  _resident(w_in.shape)],
        out_specs=[pl.BlockSpec((tm, na), lambda i: (i, 0)),
                   pl.BlockSpec((tm, nb), lambda i: (i, 0))],
        out_shape=[jax.ShapeDtypeStruct((t, na), BF16),
                   jax.ShapeDtypeStruct((t, nb), F32)],
        compiler_params=_cparams(("parallel",)),
        name="even_in_proj",
    )(h, gain, w_in)


def _build_bias(rb_ref, band_ref, bias_scr, h, tq):
    far_l = rb_ref[(REL_BUCKETS // 2 - 1) * DA_HEADS + h]
    far_r = rb_ref[(REL_BUCKETS - 1) * DA_HEADS + h]
    nt = tq // LANES
    for dd in range(5):
        for r in range(nt):
            for c in range(nt):
                d = nt * (dd - 2) + c - r
                if abs(d) <= 1:
                    tile = band_ref[:, (d + 1) * LANES:(d + 2) * LANES]
                else:
                    tile = jnp.full((LANES, LANES), far_l if d < 0 else far_r, F32)
                bias_scr[dd, r * LANES:(r + 1) * LANES, c * LANES:(c + 1) * LANES] = tile * LOG2E


def _attn_steps(lamv_ref, q_ref, k_ref, v_ref, sub_ref, o_ref, bias_scr, qi,
                *, tq, seq, lambda_init):
    tk = tq
    nkb = seq // tk
    q = q_ref[...]
    lane = lax.broadcasted_iota(jnp.int32, q.shape, 1)
    zero = jnp.zeros_like(q)
    qq = jnp.concatenate([jnp.where(lane < DA_QK_DIM, q, zero),
                          jnp.where(lane < DA_QK_DIM, zero, q)], axis=0)

    def scores(kb):
        bt = bias_scr[jnp.clip(kb - qi, -2, 2) + 2]
        s = _dot_nt(qq, k_ref[kb * tk:(kb + 1) * tk, :]) + jnp.concatenate([bt, bt], axis=0)
        return s.astype(BF16)

    ones = jnp.ones((tk, LANES), BF16)
    m = jnp.full((2 * tq, 1), -jnp.inf, F32)
    acc = jnp.zeros((2 * tq, DA_V_DIM + LANES), F32)
    s = scores(0)
    yield
    for kb in range(nkb):
        s_next = scores(kb + 1) if kb + 1 < nkb else None
        m_new = jnp.maximum(m, jnp.max(s, axis=-1, keepdims=True).astype(F32))
        alpha = jnp.exp2(m - m_new)
        p = jnp.exp2(s - m_new.astype(BF16))
        v_ext = jnp.concatenate([v_ref[kb * tk:(kb + 1) * tk, :], ones], axis=1)
        acc = alpha * acc + _dot(p, v_ext)
        m = m_new
        s = s_next
        yield

    lamv = lamv_ref[...]
    lam = (jnp.exp(jnp.sum(lamv[0:1] * lamv[1:2], axis=-1, keepdims=True))
           - jnp.exp(jnp.sum(lamv[2:3] * lamv[3:4], axis=-1, keepdims=True)) + lambda_init)
    sm = acc[:, :DA_V_DIM] * pl.reciprocal(acc[:, DA_V_DIM:], approx=True)
    o = sm[:tq] - lam * sm[tq:]
    o_ref[...] = (_rms(o, sub_ref[...]) * (1.0 - lambda_init)).astype(BF16)


def _hg_sum_matrix(reverse):
    c = HG_CHUNK
    r = np.arange(c)
    blocks = []
    if not reverse:
        blocks.append(r[None, :] <= r[:, None])
        blocks.append(r[None, :] > r[:, None])
    else:
        blocks.append(r[None, :] >= r[:, None])
        blocks.append(r[None, :] < r[:, None])
    blocks.append(np.ones((HG_TOT_ROWS, c), bool))
    for lvl in range(HG_LEVELS):
        hs = 1 << lvl
        base = r & ~(2 * hs - 1)
        upper = ((r >> lvl) & 1) == 1
        m = np.zeros((c, c), bool)
        for i in range(c):
            if not reverse:
                bnd = base[i] + hs - 1
                lo, hi = (bnd + 1, i) if upper[i] else (i + 1, bnd)
            else:
                bnd = base[i] + hs
                lo, hi = (bnd, i - 1) if upper[i] else (i, bnd - 1)
            m[i, lo:hi + 1] = True
        blocks.append(m)
    m = np.concatenate(blocks, axis=0).astype(np.float32)
    return np.concatenate([m, m, m], axis=1)


def _hg_steps(q_ref, z_ref, v_ref, lb_ref, m_ref, o_ref, st_ref, inter_ref, nsc, reverse, redo):
    c = HG_CHUNK
    q = q_ref[...]
    z = z_ref[...]
    v = v_ref[...].astype(BF16)
    lb = lb_ref[...]

    z2 = z * LOG2E
    e = jnp.exp2(-jnp.abs(z2))
    r = pl.reciprocal(1.0 + e, approx=True)
    pos = z >= 0
    sig = jnp.where(pos, r, e * r)
    kk = (1.0 - lb) * jnp.where(pos, e * r, r)
    has_lb = lb > 0
    log_f = (jnp.where(has_lb, 0.0, jnp.minimum(z2, 0.0))
             + jnp.log2(jnp.where(has_lb, lb + (1.0 - lb) * sig, r)))
    yield

    def side_by_side(x):
        return jnp.concatenate([x[i * c:(i + 1) * c] for i in range(nsc)], axis=1)

    def chunk(x, i):
        return x[:, i * HG_K:(i + 1) * HG_K]

    l1 = log_f.astype(BF16)
    r1 = log_f - l1.astype(F32)
    l2 = r1.astype(BF16)
    l3 = (r1 - l2.astype(F32)).astype(BF16)
    lcat = jnp.concatenate([side_by_side(l1), side_by_side(l2), side_by_side(l3)], axis=0)
    sums = _dot(m_ref[0:HG_MAIN_ROWS, :], lcat)
    cum = sums[0:c]
    rest = sums[c:2 * c]
    tot = sums[2 * c:2 * c + 1]
    yield

    ql = side_by_side(q)
    kl = side_by_side(kk)
    qe = (ql * jnp.exp2(cum)).astype(BF16)
    kd = (kl * jnp.exp2(rest)).astype(BF16)
    st_decay = jnp.exp2(tot)

    ri = lax.broadcasted_iota(jnp.int32, (c, c), 0)
    ci = lax.broadcasted_iota(jnp.int32, (c, c), 1)

    mid = cum[c // 2:c // 2 + 1]
    qx = (ql * jnp.exp2(cum - mid)).astype(BF16)
    kx = (kl * jnp.exp2(mid - cum)).astype(BF16)
    yield
    causal = (ri <= ci) if reverse else (ri >= ci)
    o_intra = [_dot(jnp.where(causal, _dot_nt(chunk(qx, i), chunk(kx, i)), 0.0).astype(BF16),
                    v[i * c:(i + 1) * c]) for i in range(nsc)]
    st_add =[_dot_tn(v[i * c:(i + 1) * c], chunk(kd, i)) for i in range(nsc)]
    yield
    st = st_ref[...]
    for i in (reversed(range(nsc)) if reverse else range(nsc)):
        inter = _dot_nt(chunk(qe, i), st.astype(BF16))
        inter_ref[i * c:(i + 1) * c, :] = inter
        o_ref[i * c:(i + 1) * c, :] = o_intra[i] + inter
        st = st * chunk(st_decay, i) + st_add[i]
    st_ref[...] = st

    def redo_if_unsafe():
        @pl.when(jnp.min(tot) < HG_FAST_LOG2_LIMIT)
        def _():
            lsum = _dot(m_ref[HG_MAIN_ROWS:HG_MAIN_ROWS + HG_LEVELS * c, :], lcat)
            row = lax.broadcasted_iota(jnp.int32, ql.shape, 0)
            qb = ql.astype(BF16)
            kb = kl.astype(BF16)
            att = [jnp.where(ri == ci, _dot_nt(chunk(qb, i), chunk(kb, i)), 0.0)
                   for i in range(nsc)]
            for lvl in range(HG_LEVELS):
                dec = jnp.exp2(lsum[lvl * c:(lvl + 1) * c])
                later = ((row >> lvl) & 1) == 1
                q_rows = jnp.logical_not(later) if reverse else later
                x = jnp.where(q_rows, ql, kl) * dec
                qt = jnp.where(q_rows, x, 0.0).astype(BF16)
                kt = jnp.where(q_rows, 0.0, x).astype(BF16)
                same = (ri >> (lvl + 1)) == (ci >> (lvl + 1))
                att = [att[i] + jnp.where(same, _dot_nt(chunk(qt, i), chunk(kt, i)), 0.0)
                       for i in range(nsc)]
            for i in range(nsc):
                rows = slice(i * c, (i + 1) * c)
                o_ref[rows, :] = _dot(att[i].astype(BF16), v[rows]) + inter_ref[rows, :]

    redo.append(redo_if_unsafe)


def _even_core_kernel(rb_ref, lamv_ref, q_ref, k_ref, v_ref, band_ref, sub_ref,
                      qf_ref, zf_ref, vf_ref, qb_ref, zb_ref, vb_ref, lbf_ref, lbb_ref,
                      mf_ref, mb_ref, oa_ref, of_ref, ob_ref,
                      bias_scr, sf_ref, sb_ref, if_ref, ib_ref, *, tb, seq, lambda_init):
    h = pl.program_id(0)
    b = pl.program_id(1)
    blk = pl.program_id(2)

    @pl.when(jnp.logical_and(b == 0, blk == 0))
    def _():
        _build_bias(rb_ref, band_ref, bias_scr, h, tb)

    @pl.when(blk == 0)
    def _():
        sf_ref[...] = jnp.zeros_like(sf_ref)
        sb_ref[...] = jnp.zeros_like(sb_ref)

    nsc = tb // HG_CHUNK
    redo = []
    attn = _attn_steps(lamv_ref, q_ref, k_ref, v_ref, sub_ref, oa_ref, bias_scr, blk,
                       tq=tb, seq=seq, lambda_init=lambda_init)
    hgrn = [_hg_steps(qf_ref, zf_ref, vf_ref, lbf_ref, mf_ref, of_ref, sf_ref, if_ref,
                      nsc, False, redo),
            _hg_steps(qb_ref, zb_ref, vb_ref, lbb_ref, mb_ref, ob_ref, sb_ref, ib_ref,
                      nsc, True, redo)]
    hg_chain = itertools.chain(*hgrn)
    for _ in attn:
        next(hg_chain, None)
    for _ in hg_chain:
        pass
    for fn in redo:
        fn()


def _even_core_call(qkv, hg, band, rel_bias, lamv, subln, lb_f, lb_b, lambda_init):
    b, s, _ = qkv.shape
    tb = min(CORE_BLOCK, s)
    nb = s // tb
    hh = HG_HEADS
    assert DA_HEADS == HG_HEADS
    mf = jnp.asarray(_hg_sum_matrix(False), BF16)
    mb = jnp.asarray(_hg_sum_matrix(True), BF16)

    def fwd(col):
        return pl.BlockSpec((None, tb, HG_K), lambda h, bb, i: (bb, i, col * hh + h))

    def bwd(col):
        return pl.BlockSpec((None, tb, HG_K), lambda h, bb, i: (bb, nb - 1 - i, col * hh + h))

    lb_spec = pl.BlockSpec((None, 1, HG_K), lambda h, bb, i: (h, 0, 0))
    kern = functools.partial(_even_core_kernel, tb=tb, seq=s, lambda_init=lambda_init)
    return pl.pallas_call(
        kern,
        grid=(hh, b, nb),
        in_specs=[pl.BlockSpec(memory_space=pltpu.SMEM),
                  _resident((4, DA_QK_DIM)),
                  pl.BlockSpec((None, tb, DA_V_DIM), lambda h, bb, i: (bb, i, h)),
                  pl.BlockSpec((None, s, DA_V_DIM), lambda h, bb, i: (bb, 0, hh + h)),
                  pl.BlockSpec((None, s, DA_V_DIM), lambda h, bb, i: (bb, 0, 2 * hh + h)),
                  pl.BlockSpec((None, LANES, 3 * LANES), lambda h, bb, i: (h, 0, 0)),
                  _resident((1, DA_V_DIM)),
                  fwd(0), fwd(1), fwd(3), bwd(0), bwd(2), bwd(3), lb_spec, lb_spec,
                  _resident(mf.shape), _resident(mb.shape)],
        out_specs=[pl.BlockSpec((None, tb, DA_V_DIM), lambda h, bb, i: (bb, i, h)),
                   pl.BlockSpec((None, tb, HG_V), lambda h, bb, i: (bb, i, h)),
                   pl.BlockSpec((None, tb, HG_V), lambda h, bb, i: (bb, nb - 1 - i, h))],
        out_shape=[jax.ShapeDtypeStruct((b, s, DA_WIDTH), BF16),
                   jax.ShapeDtypeStruct((b, s, HG_WIDTH), F32),
                   jax.ShapeDtypeStruct((b, s, HG_WIDTH), F32)],
        scratch_shapes=[pltpu.VMEM((5, tb, tb), F32),
                        pltpu.VMEM((HG_V, HG_K), F32), pltpu.VMEM((HG_V, HG_K), F32),
                        pltpu.VMEM((tb, HG_V), F32), pltpu.VMEM((tb, HG_V), F32)],
        compiler_params=_cparams(("arbitrary", "arbitrary", "arbitrary")),
        name="even_core",
    )(rel_bias.reshape(-1), lamv, qkv, qkv, qkv, band, subln,
      hg, hg, hg, hg, hg, hg, lb_f.reshape(hh, 1, HG_K), lb_b.reshape(hh, 1, HG_K), mf, mb)


def _even_out_steps(oa_ref, of_ref, ob_ref, gate_ref, h_ref, gn_ref, wo_ref, rows):
    ob = of_ref[rows, :] + ob_ref[rows, :]
    parts = [oa_ref[rows, :]]
    for hh in range(HG_HEADS):
        cols = slice(hh * HG_V, (hh + 1) * HG_V)
        y = _rms(ob[:, cols], gn_ref[...]) * _silu(gate_ref[rows, cols])
        parts.append(y.astype(BF16))
    x = h_ref[rows, :] + _dot(jnp.concatenate(parts, axis=-1), wo_ref[...])
    yield
    return x


def _even_tail_kernel(oa_ref, of_ref, ob_ref, gate_ref, h_ref, gn_ref, wo_ref,
                      g_ref, win_ref, wout_ref, fg_ref, o_ref, *, final):
    _interleave([_ffn_steps(_even_out_steps(oa_ref, of_ref, ob_ref, gate_ref, h_ref, gn_ref,
                                            wo_ref, rows),
                            g_ref, win_ref, wout_ref, fg_ref, o_ref, rows, final)
                 for rows in _half_tiles(h_ref.shape[0])])


def _even_tail_call(o_a, o_f, o_b, hg, h, hg_norm, w_o, gain, w_in, w_out, final_gain, final):
    t = h.shape[0]
    tm = min(ROW_TILE, t)
    gate_blk = hg.shape[1] // HG_WIDTH - 1
    return pl.pallas_call(
        functools.partial(_even_tail_kernel, final=final),
        grid=(t // tm,),
        in_specs=[pl.BlockSpec((tm, DA_WIDTH), lambda i: (i, 0)),
                  pl.BlockSpec((tm, HG_WIDTH), lambda i: (i, 0)),
                  pl.BlockSpec((tm, HG_WIDTH), lambda i: (i, 0)),
                  pl.BlockSpec((tm, HG_WIDTH), lambda i: (i, gate_blk)),
                  pl.BlockSpec((tm, D_MODEL), lambda i: (i, 0)),
                  _resident((1, HG_V)),
                  _resident(w_o.shape)] + _ffn_specs(w_in, w_out),
        out_specs=pl.BlockSpec((tm, D_MODEL), lambda i: (i, 0)),
        out_shape=jax.ShapeDtypeStruct((t, D_MODEL), F32),
        compiler_params=_cparams(("parallel",)),
        name="even_out_ffn",
    )(o_a, o_f, o_b, hg, h, hg_norm, w_o, gain, w_in, w_out, final_gain)


def _odd_steps(h_ref, g_ref, win_ref, sgn_ref, sgw_ref, sgb_ref, wout_ref, o_ref, vm_ref, r0, nr):
    rows = slice(r0, r0 + nr)
    x = h_ref[rows, :]
    hn = _rms(x, g_ref[...]).astype(BF16)
    y = _dot(hn, win_ref[...])
    yield
    y = 0.5 * y * (1.0 + lax.erf(y * math.sqrt(0.5)))
    u = y[:, :D_MODEL]
    v = _rms(y[:, D_MODEL:], sgn_ref[...]).astype(BF16)
    yield
    nch = nr // SG_CHUNK
    for g in range(SG_GROUPS):
        cols = slice(g * SG_GROUP_DIM, (g + 1) * SG_GROUP_DIM)
        vg = jnp.concatenate([v[c * SG_CHUNK:(c + 1) * SG_CHUNK, cols] for c in range(nch)], axis=1)
        mixed = _dot(sgw_ref[g], vg)
        for c in range(nch):
            vm_ref[r0 + c * SG_CHUNK:r0 + (c + 1) * SG_CHUNK, cols] = (
                mixed[:, c * SG_GROUP_DIM:(c + 1) * SG_GROUP_DIM] + sgb_ref[:, cols])
    yield
    t = (u * vm_ref[rows, :]).astype(BF16)
    o_ref[rows, :] = x + _dot(t, wout_ref[...])


def _odd_kernel(h_ref, g_ref, win_ref, sgn_ref, sgw_ref, sgb_ref, wout_ref, o_ref, vm_ref, *, tm):
    nr = max(tm // 2, SG_CHUNK)
    _interleave([_odd_steps(h_ref, g_ref, win_ref, sgn_ref, sgw_ref, sgb_ref, wout_ref,
                            o_ref, vm_ref, r0, nr) for r0 in range(0, tm, nr)])


def _odd_call(h, gain, w_in, sg_norm, sg_w, sg_b_full, w_out):
    t = h.shape[0]
    tm = min(ROW_TILE, t)
    return pl.pallas_call(
        functools.partial(_odd_kernel, tm=tm),
        grid=(t // tm,),
        in_specs=[pl.BlockSpec((tm, D_MODEL), lambda i: (i, 0)),
                  _resident((1, D_MODEL)),
                  _resident(w_in.shape),
                  _resident((1, D_MODEL)),
                  _resident(sg_w.shape),
                  _resident(sg_b_full.shape),
                  _resident(w_out.shape)],
        out_specs=pl.BlockSpec((tm, D_MODEL), lambda i: (i, 0)),
        out_shape=jax.ShapeDtypeStruct((t, D_MODEL), F32),
        scratch_shapes=[pltpu.VMEM((tm, D_MODEL), F32)],
        compiler_params=_cparams(("parallel",)),
        name="odd_mixer",
    )(h, gain, w_in, sg_norm, sg_w, sg_b_full, w_out)


def _ffn_steps(x_steps, g_ref, win_ref, wout_ref, fg_ref, o_ref, rows, final):
    x = yield from x_steps
    hn = _rms(x, g_ref[...]).astype(BF16)
    gate = _dot(hn, win_ref[:, :D_FF])
    up = _dot(hn, win_ref[:, D_FF:])
    yield
    act = (_silu(gate) * up).astype(BF16)
    acc = x + _dot(act, wout_ref[...])
    if final:
        acc = _rms(acc, fg_ref[...])
    o_ref[rows, :] = acc


def _load_rows(ref, rows):
    return ref[rows, :]
    yield


def _ffn_kernel(h_ref, g_ref, win_ref, wout_ref, fg_ref, o_ref, *, final):
    _interleave([_ffn_steps(_load_rows(h_ref, rows), g_ref, win_ref, wout_ref, fg_ref, o_ref,
                            rows, final) for rows in _half_tiles(h_ref.shape[0])])


def _ffn_specs(w_in, w_out):
    return [_resident((1, D_MODEL)), _resident(w_in.shape), _resident(w_out.shape),
            _resident((1, D_MODEL))]


def _ffn_call(h, gain, w_in, w_out, final_gain, final):
    t = h.shape[0]
    tm = min(ROW_TILE, t)
    return pl.pallas_call(
        functools.partial(_ffn_kernel, final=final),
        grid=(t // tm,),
        in_specs=[pl.BlockSpec((tm, D_MODEL), lambda i: (i, 0))] + _ffn_specs(w_in, w_out),
        out_specs=pl.BlockSpec((tm, D_MODEL), lambda i: (i, 0)),
        out_shape=jax.ShapeDtypeStruct((t, D_MODEL), F32),
        compiler_params=_cparams(("parallel",)),
        name="swiglu_ffn",
    )(h, gain, w_in, w_out, final_gain)


def _lower_bounds(p):
    lb = jnp.cumsum(jax.nn.softmax(p.astype(F32), axis=0), axis=0)
    return lb - lb[:1]


def kernel(x, rel_bias, norm_mix, norm_ffn, norm_final, w_in_even, w_out_even,
           lambda_q1, lambda_k1, lambda_q2, lambda_k2, da_subln, hg_lb_fwd, hg_lb_bwd, hg_norm,
           w_in_odd, sg_norm, sg_w, sg_b, w_out_odd, w_ffn_in, w_ffn_out):
    b, s, d = x.shape
    t = b * s
    lb_f = _lower_bounds(hg_lb_fwd)
    lb_b = _lower_bounds(hg_lb_bwd)
    band = _band_call(rel_bias.astype(F32))
    n_attn = 3 * DA_WIDTH
    h = x.reshape(t, d)
    for l in range(DEPTH):
        gain = norm_mix[l].reshape(1, d)
        ffn = (norm_ffn[l].reshape(1, d), w_ffn_in[l].astype(BF16), w_ffn_out[l].astype(BF16),
               norm_final.reshape(1, d), l == DEPTH - 1)
        if l % 2 == 0:
            e = l // 2
            lambda_init = 0.8 - 0.6 * math.exp(-0.3 * l)
            w_in = w_in_even[e].astype(BF16)
            qkv, hg = _even_in_call(h, gain, w_in, n_attn)
            lamv = jnp.stack([lambda_q1[e], lambda_k1[e], lambda_q2[e], lambda_k2[e]]).astype(F32)
            o_a, o_f, o_b = _even_core_call(
                qkv.reshape(b, s, n_attn), hg.reshape(b, s, -1), band, rel_bias.astype(F32), lamv,
                da_subln[e].reshape(1, DA_V_DIM), lb_f[e], lb_b[e], lambda_init)
            h = _even_tail_call(o_a.reshape(t, DA_WIDTH), o_f.reshape(t, HG_WIDTH),
                                o_b.reshape(t, HG_WIDTH), hg, h,
                                hg_norm[e].reshape(1, HG_V), w_out_even[e].astype(BF16), *ffn)
        else:
            o = l // 2
            sg_b_full = jnp.repeat(sg_b[o].T.astype(F32), SG_GROUP_DIM, axis=1)
            h = _odd_call(h, gain, w_in_odd[o].astype(BF16), sg_norm[o].reshape(1, d),
                          sg_w[o].astype(BF16), sg_b_full, w_out_odd[o].astype(BF16))
            h = _ffn_call(h, *ffn)
    return h.reshape(b, s, d)
```

```python
import functools
import itertools
import math

import numpy as np
import jax
import jax.numpy as jnp
from jax import lax
from jax.experimental import pallas as pl
from jax.experimental.pallas import tpu as pltpu

F32 = jnp.float32
BF16 = jnp.bfloat16

D_MODEL = 1024
DEPTH = 4
DA_HEADS = 4
DA_QK_DIM = 64
DA_V_DIM = 128
DA_WIDTH = DA_HEADS * DA_V_DIM
HG_HEADS = 4
HG_K = 128
HG_V = 128
HG_WIDTH = HG_HEADS * HG_V
SG_GROUPS = 8
SG_CHUNK = 128
SG_GROUP_DIM = D_MODEL // SG_GROUPS
REL_BUCKETS = 32
REL_MAX_DIST = 128
D_FF = 2816
EPS = 1e-6

LANES = 128
VMEM_LIMIT = 56 * 1024 * 1024

ROW_TILE = 512
CORE_BLOCK = 512
HG_CHUNK = 64
HG_LEVELS = 6
HG_TOT_ROWS = 16
HG_MAIN_ROWS = 2 * HG_CHUNK + HG_TOT_ROWS
HG_FAST_LOG2_LIMIT = -100.0

LOG2E = math.log2(math.e)
Q_SCALE = DA_QK_DIM ** -0.5 * LOG2E


def _cparams(sem):
    return pltpu.CompilerParams(dimension_semantics=sem, vmem_limit_bytes=VMEM_LIMIT)


def _resident(shape):
    nd = len(shape)
    return pl.BlockSpec(shape, lambda *_: (0,) * nd, pipeline_mode=pl.Buffered(1))


def _rms(x, gain):
    ms = jnp.mean(x * x, axis=-1, keepdims=True)
    return x * lax.rsqrt(ms + EPS) * gain


def _silu(x):
    return x * pl.reciprocal(1.0 + jnp.exp(-x), approx=True)


def _dot(a, b):
    return jnp.dot(a, b, preferred_element_type=F32)


def _dot_nt(a, b):
    return lax.dot_general(a, b, (((1,), (1,)), ((), ())), preferred_element_type=F32)


def _dot_tn(a, b):
    return lax.dot_general(a, b, (((0,), (0,)), ((), ())), preferred_element_type=F32)


def _interleave(gens):
    live = []
    pending = list(gens)
    done = object()
    while live or pending:
        if pending:
            live.append(pending.pop(0))
        live = [g for g in live if next(g, done) is not done]


def _half_tiles(tm):
    nr = tm // 2 if tm % 16 == 0 else tm
    return [slice(r0, r0 + nr) for r0 in range(0, tm, nr)]


def _band_kernel(rb_ref, o_ref):
    h = pl.program_id(0)
    shape = (LANES, 3 * LANES)
    qi = lax.broadcasted_iota(jnp.int32, shape, 0)
    xi = lax.broadcasted_iota(jnp.int32, shape, 1)
    rel = xi - LANES - qi
    half = REL_BUCKETS // 2
    max_exact = half // 2
    ret = jnp.where(rel > 0, half, 0)
    n = jnp.abs(rel)
    nf = jnp.maximum(n, 1).astype(F32)
    large = max_exact + (jnp.log(nf / max_exact) / math.log(REL_MAX_DIST / max_exact)
                         * (half - max_exact)).astype(jnp.int32)
    large = jnp.minimum(large, half - 1)
    bucket = ret + jnp.where(n < max_exact, n, large)
    out = jnp.zeros(shape, F32)
    for bkt in range(REL_BUCKETS):
        out = jnp.where(bucket == bkt, rb_ref[bkt * DA_HEADS + h], out)
    o_ref[...] = out


def _band_call(rel_bias):
    return pl.pallas_call(
        _band_kernel,
        grid=(DA_HEADS,),
        in_specs=[pl.BlockSpec(memory_space=pltpu.SMEM)],
        out_specs=pl.BlockSpec((None, LANES, 3 * LANES), lambda h: (h, 0, 0)),
        out_shape=jax.ShapeDtypeStruct((DA_HEADS, LANES, 3 * LANES), F32),
        name="rel_bias_band",
    )(rel_bias.reshape(-1))


def _even_in_steps(x_ref, g_ref, w_ref, oa_ref, ob_ref, rows):
    na = oa_ref.shape[1]
    hn = _rms(x_ref[rows, :], g_ref[...]).astype(BF16)
    oa = _dot(hn, w_ref[:, :na])
    ob = _dot(hn, w_ref[:, na:])
    yield
    oa_ref[rows, :DA_WIDTH] = (oa[:, :DA_WIDTH] * Q_SCALE).astype(BF16)
    oa_ref[rows, DA_WIDTH:] = oa[:, DA_WIDTH:].astype(BF16)
    ob_ref[rows, :HG_WIDTH] = _silu(ob[:, :HG_WIDTH])
    ob_ref[rows, HG_WIDTH:] = ob[:, HG_WIDTH:]


def _even_in_kernel(x_ref, g_ref, w_ref, oa_ref, ob_ref):
    _interleave([_even_in_steps(x_ref, g_ref, w_ref, oa_ref, ob_ref, rows)
                 for rows in _half_tiles(x_ref.shape[0])])


def _even_in_call(h, gain, w_in, na):
    t = h.shape[0]
    tm = min(ROW_TILE, t)
    nb = w_in.shape[1] - na
    return pl.pallas_call(
        _even_in_kernel,
        grid=(t // tm,),
        in_specs=[pl.BlockSpec((tm, D_MODEL), lambda i: (i, 0)),
                  _resident((1, D_MODEL)),
                  _resident(w_in.shape)],
        out_specs=[pl.BlockSpec((tm, na), lambda i: (i, 0)),
                   pl.BlockSpec((tm, nb), lambda i: (i, 0))],
        out_shape=[jax.ShapeDtypeStruct((t, na), BF16),
                   jax.ShapeDtypeStruct((t, nb), F32)],
        compiler_params=_cparams(("parallel",)),
        name="even_in_proj",
    )(h, gain, w_in)


def _build_bias(rb_ref, band_ref, bias_scr, h, tq):
    far_l = rb_ref[(REL_BUCKETS // 2 - 1) * DA_HEADS + h]
    far_r = rb_ref[(REL_BUCKETS - 1) * DA_HEADS + h]
    nt = tq // LANES
    for dd in range(5):
        for r in range(nt):
            for c in range(nt):
                d = nt * (dd - 2) + c - r
                if abs(d) <= 1:
                    tile = band_ref[:, (d + 1) * LANES:(d + 2) * LANES]
                else:
                    tile = jnp.full((LANES, LANES), far_l if d < 0 else far_r, F32)
                bias_scr[dd, r * LANES:(r + 1) * LANES, c * LANES:(c + 1) * LANES] = tile * LOG2E


def _attn_steps(lamv_ref, q_ref, k_ref, v_ref, sub_ref, o_ref, bias_scr, qi,
                *, tq, seq, lambda_init):
    tk = tq
    nkb = seq // tk
    q = q_ref[...]
    lane = lax.broadcasted_iota(jnp.int32, q.shape, 1)
    zero = jnp.zeros_like(q)
    qq = jnp.concatenate([jnp.where(lane < DA_QK_DIM, q, zero),
                          jnp.where(lane < DA_QK_DIM, zero, q)], axis=0)

    def scores(kb):
        bt = bias_scr[jnp.clip(kb - qi, -2, 2) + 2]
        s = _dot_nt(qq, k_ref[kb * tk:(kb + 1) * tk, :]) + jnp.concatenate([bt, bt], axis=0)
        return s.astype(BF16)

    ones = jnp.ones((tk, LANES), BF16)
    m = jnp.full((2 * tq, 1), -jnp.inf, F32)
    acc = jnp.zeros((2 * tq, DA_V_DIM + LANES), F32)
    s = scores(0)
    yield
    for kb in range(nkb):
        s_next = scores(kb + 1) if kb + 1 < nkb else None
        m_new = jnp.maximum(m, jnp.max(s, axis=-1, keepdims=True).astype(F32))
        alpha = jnp.exp2(m - m_new)
        p = jnp.exp2(s - m_new.astype(BF16))
        v_ext = jnp.concatenate([v_ref[kb * tk:(kb + 1) * tk, :], ones], axis=1)
        acc = alpha * acc + _dot(p, v_ext)
        m = m_new
        s = s_next
        yield

    lamv = lamv_ref[...]
    lam = (jnp.exp(jnp.sum(lamv[0:1] * lamv[1:2], axis=-1, keepdims=True))
           - jnp.exp(jnp.sum(lamv[2:3] * lamv[3:4], axis=-1, keepdims=True)) + lambda_init)
    sm = acc[:, :DA_V_DIM] * pl.reciprocal(acc[:, DA_V_DIM:], approx=True)
    o = sm[:tq] - lam * sm[tq:]
    o_ref[...] = (_rms(o, sub_ref[...]) * (1.0 - lambda_init)).astype(BF16)


def _hg_sum_matrix(reverse):
    c = HG_CHUNK
    r = np.arange(c)
    blocks = []
    if not reverse:
        blocks.append(r[None, :] <= r[:, None])
        blocks.append(r[None, :] > r[:, None])
    else:
        blocks.append(r[None, :] >= r[:, None])
        blocks.append(r[None, :] < r[:, None])
    blocks.append(np.ones((HG_TOT_ROWS, c), bool))
    for lvl in range(HG_LEVELS):
        hs = 1 << lvl
        base = r & ~(2 * hs - 1)
        upper = ((r >> lvl) & 1) == 1
        m = np.zeros((c, c), bool)
        for i in range(c):
            if not reverse:
                bnd = base[i] + hs - 1
                lo, hi = (bnd + 1, i) if upper[i] else (i + 1, bnd)
            else:
                bnd = base[i] + hs
                lo, hi = (bnd, i - 1) if upper[i] else (i, bnd - 1)
            m[i, lo:hi + 1] = True
        blocks.append(m)
    m = np.concatenate(blocks, axis=0).astype(np.float32)
    return np.concatenate([m, m, m], axis=1)


def _hg_steps(q_ref, z_ref, v_ref, lb_ref, m_ref, o_ref, st_ref, inter_ref, nsc, reverse, redo):
    c = HG_CHUNK
    q = q_ref[...]
    z = z_ref[...]
    v = v_ref[...].astype(BF16)
    lb = lb_ref[...]

    z2 = z * LOG2E
    e = jnp.exp2(-jnp.abs(z2))
    r = pl.reciprocal(1.0 + e, approx=True)
    pos = z >= 0
    sig = jnp.where(pos, r, e * r)
    kk = (1.0 - lb) * jnp.where(pos, e * r, r)
    has_lb = lb > 0
    log_f = (jnp.where(has_lb, 0.0, jnp.minimum(z2, 0.0))
             + jnp.log2(jnp.where(has_lb, lb + (1.0 - lb) * sig, r)))
    yield

    def side_by_side(x):
        return jnp.concatenate([x[i * c:(i + 1) * c] for i in range(nsc)], axis=1)

    def chunk(x, i):
        return x[:, i * HG_K:(i + 1) * HG_K]

    l1 = log_f.astype(BF16)
    r1 = log_f - l1.astype(F32)
    l2 = r1.astype(BF16)
    l3 = (r1 - l2.astype(F32)).astype(BF16)
    lcat = jnp.concatenate([side_by_side(l1), side_by_side(l2), side_by_side(l3)], axis=0)
    sums = _dot(m_ref[0:HG_MAIN_ROWS, :], lcat)
    cum = sums[0:c]
    rest = sums[c:2 * c]
    tot = sums[2 * c:2 * c + 1]
    yield

    ql = side_by_side(q)
    kl = side_by_side(kk)
    qe = (ql * jnp.exp2(cum)).astype(BF16)
    kd = (kl * jnp.exp2(rest)).astype(BF16)
    st_decay = jnp.exp2(tot)

    ri = lax.broadcasted_iota(jnp.int32, (c, c), 0)
    ci = lax.broadcasted_iota(jnp.int32, (c, c), 1)

    mid = cum[c // 2:c // 2 + 1]
    qx = (ql * jnp.exp2(cum - mid)).astype(BF16)
    kx = (kl * jnp.exp2(mid - cum)).astype(BF16)
    yield
    causal = (ri <= ci) if reverse else (ri >= ci)
    o_intra = [_dot(jnp.where(causal, _dot_nt(chunk(qx, i), chunk(kx, i)), 0.0).astype(BF16),
                    v[i * c:(i + 1) * c]) for i in range(nsc)]
    st_add =[_dot_tn(v[i * c:(i + 1) * c], chunk(kd, i)) for i in range(nsc)]
    yield
    st = st_ref[...]
    for i in (reversed(range(nsc)) if reverse else range(nsc)):
        inter = _dot_nt(chunk(qe, i), st.astype(BF16))
        inter_ref[i * c:(i + 1) * c, :] = inter
        o_ref[i * c:(i + 1) * c, :] = o_intra[i] + inter
        st = st * chunk(st_decay, i) + st_add[i]
    st_ref[...] = st

    def redo_if_unsafe():
        @pl.when(jnp.min(tot) < HG_FAST_LOG2_LIMIT)
        def _():
            lsum = _dot(m_ref[HG_MAIN_ROWS:HG_MAIN_ROWS + HG_LEVELS * c, :], lcat)
            row = lax.broadcasted_iota(jnp.int32, ql.shape, 0)
            qb = ql.astype(BF16)
            kb = kl.astype(BF16)
            att = [jnp.where(ri == ci, _dot_nt(chunk(qb, i), chunk(kb, i)), 0.0)
                   for i in range(nsc)]
            for lvl in range(HG_LEVELS):
                dec = jnp.exp2(lsum[lvl * c:(lvl + 1) * c])
                later = ((row >> lvl) & 1) == 1
                q_rows = jnp.logical_not(later) if reverse else later
                x = jnp.where(q_rows, ql, kl) * dec
                qt = jnp.where(q_rows, x, 0.0).astype(BF16)
                kt = jnp.where(q_rows, 0.0, x).astype(BF16)
                same = (ri >> (lvl + 1)) == (ci >> (lvl + 1))
                att = [att[i] + jnp.where(same, _dot_nt(chunk(qt, i), chunk(kt, i)), 0.0)
                       for i in range(nsc)]
            for i in range(nsc):
                rows = slice(i * c, (i + 1) * c)
                o_ref[rows, :] = _dot(att[i].astype(BF16), v[rows]) + inter_ref[rows, :]

    redo.append(redo_if_unsafe)


def _even_core_kernel(rb_ref, lamv_ref, q_ref, k_ref, v_ref, band_ref, sub_ref,
                      qf_ref, zf_ref, vf_ref, qb_ref, zb_ref, vb_ref, lbf_ref, lbb_ref,
                      mf_ref, mb_ref, oa_ref, of_ref, ob_ref,
                      bias_scr, sf_ref, sb_ref, if_ref, ib_ref, *, tb, seq, lambda_init):
    h = pl.program_id(0)
    b = pl.program_id(1)
    blk = pl.program_id(2)

    @pl.when(jnp.logical_and(b == 0, blk == 0))
    def _():
        _build_bias(rb_ref, band_ref, bias_scr, h, tb)

    @pl.when(blk == 0)
    def _():
        sf_ref[...] = jnp.zeros_like(sf_ref)
        sb_ref[...] = jnp.zeros_like(sb_ref)

    nsc = tb // HG_CHUNK
    redo = []
    attn = _attn_steps(lamv_ref, q_ref, k_ref, v_ref, sub_ref, oa_ref, bias_scr, blk,
                       tq=tb, seq=seq, lambda_init=lambda_init)
    hgrn = [_hg_steps(qf_ref, zf_ref, vf_ref, lbf_ref, mf_ref, of_ref, sf_ref, if_ref,
                      nsc, False, redo),
            _hg_steps(qb_ref, zb_ref, vb_ref, lbb_ref, mb_ref, ob_ref, sb_ref, ib_ref,
                      nsc, True, redo)]
    hg_chain = itertools.chain(*hgrn)
    for _ in attn:
        next(hg_chain, None)
    for _ in hg_chain:
        pass
    for fn in redo:
        fn()


def _even_core_call(qkv, hg, band, rel_bias, lamv, subln, lb_f, lb_b, lambda_init):
    b, s, _ = qkv.shape
    tb = min(CORE_BLOCK, s)
    nb = s // tb
    hh = HG_HEADS
    assert DA_HEADS == HG_HEADS
    mf = jnp.asarray(_hg_sum_matrix(False), BF16)
    mb = jnp.asarray(_hg_sum_matrix(True), BF16)

    def fwd(col):
        return pl.BlockSpec((None, tb, HG_K), lambda h, bb, i: (bb, i, col * hh + h))

    def bwd(col):
        return pl.BlockSpec((None, tb, HG_K), lambda h, bb, i: (bb, nb - 1 - i, col * hh + h))

    lb_spec = pl.BlockSpec((None, 1, HG_K), lambda h, bb, i: (h, 0, 0))
    kern = functools.partial(_even_core_kernel, tb=tb, seq=s, lambda_init=lambda_init)
    return pl.pallas_call(
        kern,
        grid=(hh, b, nb),
        in_specs=[pl.BlockSpec(memory_space=pltpu.SMEM),
                  _resident((4, DA_QK_DIM)),
                  pl.BlockSpec((None, tb, DA_V_DIM), lambda h, bb, i: (bb, i, h)),
                  pl.BlockSpec((None, s, DA_V_DIM), lambda h, bb, i: (bb, 0, hh + h)),
                  pl.BlockSpec((None, s, DA_V_DIM), lambda h, bb, i: (bb, 0, 2 * hh + h)),
                  pl.BlockSpec((None, LANES, 3 * LANES), lambda h, bb, i: (h, 0, 0)),
                  _resident((1, DA_V_DIM)),
                  fwd(0), fwd(1), fwd(3), bwd(0), bwd(2), bwd(3), lb_spec, lb_spec,
                  _resident(mf.shape), _resident(mb.shape)],
        out_specs=[pl.BlockSpec((None, tb, DA_V_DIM), lambda h, bb, i: (bb, i, h)),
                   pl.BlockSpec((None, tb, HG_V), lambda h, bb, i: (bb, i, h)),
                   pl.BlockSpec((None, tb, HG_V), lambda h, bb, i: (bb, nb - 1 - i, h))],
        out_shape=[jax.ShapeDtypeStruct((b, s, DA_WIDTH), BF16),
                   jax.ShapeDtypeStruct((b, s, HG_WIDTH), F32),
                   jax.ShapeDtypeStruct((b, s, HG_WIDTH), F32)],
        scratch_shapes=[pltpu.VMEM((5, tb, tb), F32),
                        pltpu.VMEM((HG_V, HG_K), F32), pltpu.VMEM((HG_V, HG_K), F32),
                        pltpu.VMEM((tb, HG_V), F32), pltpu.VMEM((tb, HG_V), F32)],
        compiler_params=_cparams(("arbitrary", "arbitrary", "arbitrary")),
        name="even_core",
    )(rel_bias.reshape(-1), lamv, qkv, qkv, qkv, band, subln,
      hg, hg, hg, hg, hg, hg, lb_f.reshape(hh, 1, HG_K), lb_b.reshape(hh, 1, HG_K), mf, mb)


def _even_out_steps(oa_ref, of_ref, ob_ref, gate_ref, h_ref, gn_ref, wo_ref, rows):
    ob = of_ref[rows, :] + ob_ref[rows, :]
    parts = [oa_ref[rows, :]]
    for hh in range(HG_HEADS):
        cols = slice(hh * HG_V, (hh + 1) * HG_V)
        y = _rms(ob[:, cols], gn_ref[...]) * _silu(gate_ref[rows, cols])
        parts.append(y.astype(BF16))
    x = h_ref[rows, :] + _dot(jnp.concatenate(parts, axis=-1), wo_ref[...])
    yield
    return x


def _even_tail_kernel(oa_ref, of_ref, ob_ref, gate_ref, h_ref, gn_ref, wo_ref,
                      g_ref, win_ref, wout_ref, fg_ref, o_ref, *, final):
    _interleave([_ffn_steps(_even_out_steps(oa_ref, of_ref, ob_ref, gate_ref, h_ref, gn_ref,
                                            wo_ref, rows),
                            g_ref, win_ref, wout_ref, fg_ref, o_ref, rows, final)
                 for rows in _half_tiles(h_ref.shape[0])])


def _even_tail_call(o_a, o_f, o_b, hg, h, hg_norm, w_o, gain, w_in, w_out, final_gain, final):
    t = h.shape[0]
    tm = min(ROW_TILE, t)
    gate_blk = hg.shape[1] // HG_WIDTH - 1
    return pl.pallas_call(
        functools.partial(_even_tail_kernel, final=final),
        grid=(t // tm,),
        in_specs=[pl.BlockSpec((tm, DA_WIDTH), lambda i: (i, 0)),
                  pl.BlockSpec((tm, HG_WIDTH), lambda i: (i, 0)),
                  pl.BlockSpec((tm, HG_WIDTH), lambda i: (i, 0)),
                  pl.BlockSpec((tm, HG_WIDTH), lambda i: (i, gate_blk)),
                  pl.BlockSpec((tm, D_MODEL), lambda i: (i, 0)),
                  _resident((1, HG_V)),
                  _resident(w_o.shape)] + _ffn_specs(w_in, w_out),
        out_specs=pl.BlockSpec((tm, D_MODEL), lambda i: (i, 0)),
        out_shape=jax.ShapeDtypeStruct((t, D_MODEL), F32),
        compiler_params=_cparams(("parallel",)),
        name="even_out_ffn",
    )(o_a, o_f, o_b, hg, h, hg_norm, w_o, gain, w_in, w_out, final_gain)


def _odd_steps(h_ref, g_ref, win_ref, sgn_ref, sgw_ref, sgb_ref, wout_ref, vm_ref, rows):
    r0, nr = rows.start, rows.stop - rows.start
    x = h_ref[rows, :]
    hn = _rms(x, g_ref[...]).astype(BF16)
    y = _dot(hn, win_ref[...])
    yield
    y = 0.5 * y * (1.0 + lax.erf(y * math.sqrt(0.5)))
    u = y[:, :D_MODEL]
    v = _rms(y[:, D_MODEL:], sgn_ref[...]).astype(BF16)
    yield
    nch = nr // SG_CHUNK
    for g in range(SG_GROUPS):
        cols = slice(g * SG_GROUP_DIM, (g + 1) * SG_GROUP_DIM)
        vg = jnp.concatenate([v[c * SG_CHUNK:(c + 1) * SG_CHUNK, cols] for c in range(nch)], axis=1)
        mixed = _dot(sgw_ref[g], vg)
        for c in range(nch):
            vm_ref[r0 + c * SG_CHUNK:r0 + (c + 1) * SG_CHUNK, cols] = (
                mixed[:, c * SG_GROUP_DIM:(c + 1) * SG_GROUP_DIM] + sgb_ref[:, cols])
    yield
    t = (u * vm_ref[rows, :]).astype(BF16)
    x = x + _dot(t, wout_ref[...])
    yield
    return x


def _odd_layer_kernel(h_ref, gm_ref, wmi_ref, sgn_ref, sgw_ref, sgb_ref, wmo_ref,
                      g_ref, win_ref, wout_ref, fg_ref, o_ref, vm_ref, *, final):
    tm = h_ref.shape[0]
    halves = _half_tiles(tm) if tm // 2 % SG_CHUNK == 0 else [slice(0, tm)]
    _interleave([_ffn_steps(_odd_steps(h_ref, gm_ref, wmi_ref, sgn_ref, sgw_ref, sgb_ref,
                                       wmo_ref, vm_ref, rows),
                            g_ref, win_ref, wout_ref, fg_ref, o_ref, rows, final)
                 for rows in halves])


def _odd_layer_call(h, gain, w_mix_in, sg_norm, sg_w, sg_b_full, w_mix_out,
                    ffn_gain, w_in, w_out, final_gain, final):
    t = h.shape[0]
    tm = min(ROW_TILE, t)
    return pl.pallas_call(
        functools.partial(_odd_layer_kernel, final=final),
        grid=(t // tm,),
        in_specs=[pl.BlockSpec((tm, D_MODEL), lambda i: (i, 0)),
                  _resident((1, D_MODEL)),
                  _resident(w_mix_in.shape),
                  _resident((1, D_MODEL)),
                  _resident(sg_w.shape),
                  _resident(sg_b_full.shape),
                  _resident(w_mix_out.shape)] + _ffn_specs(w_in, w_out),
        out_specs=pl.BlockSpec((tm, D_MODEL), lambda i: (i, 0)),
        out_shape=jax.ShapeDtypeStruct((t, D_MODEL), F32),
        scratch_shapes=[pltpu.VMEM((tm, D_MODEL), F32)],
        compiler_params=_cparams(("parallel",)),
        name="odd_mixer_ffn",
    )(h, gain, w_mix_in, sg_norm, sg_w, sg_b_full, w_mix_out, ffn_gain, w_in, w_out, final_gain)


def _ffn_steps(x_steps, g_ref, win_ref, wout_ref, fg_ref, o_ref, rows, final):
    x = yield from x_steps
    hn = _rms(x, g_ref[...]).astype(BF16)
    gate = _dot(hn, win_ref[:, :D_FF])
    up = _dot(hn, win_ref[:, D_FF:])
    yield
    act = (_silu(gate) * up).astype(BF16)
    acc = x + _dot(act, wout_ref[...])
    if final:
        acc = _rms(acc, fg_ref[...])
    o_ref[rows, :] = acc


def _load_rows(ref, rows):
    return ref[rows, :]
    yield


def _ffn_kernel(h_ref, g_ref, win_ref, wout_ref, fg_ref, o_ref, *, final):
    _interleave([_ffn_steps(_load_rows(h_ref, rows), g_ref, win_ref, wout_ref, fg_ref, o_ref,
                            rows, final) for rows in _half_tiles(h_ref.shape[0])])


def _ffn_specs(w_in, w_out):
    return [_resident((1, D_MODEL)), _resident(w_in.shape), _resident(w_out.shape),
            _resident((1, D_MODEL))]


def _ffn_call(h, gain, w_in, w_out, final_gain, final):
    t = h.shape[0]
    tm = min(ROW_TILE, t)
    return pl.pallas_call(
        functools.partial(_ffn_kernel, final=final),
        grid=(t // tm,),
        in_specs=[pl.BlockSpec((tm, D_MODEL), lambda i: (i, 0))] + _ffn_specs(w_in, w_out),
        out_specs=pl.BlockSpec((tm, D_MODEL), lambda i: (i, 0)),
        out_shape=jax.ShapeDtypeStruct((t, D_MODEL), F32),
        compiler_params=_cparams(("parallel",)),
        name="swiglu_ffn",
    )(h, gain, w_in, w_out, final_gain)


def _lower_bounds(p):
    lb = jnp.cumsum(jax.nn.softmax(p.astype(F32), axis=0), axis=0)
    return lb - lb[:1]


def kernel(x, rel_bias, norm_mix, norm_ffn, norm_final, w_in_even, w_out_even,
           lambda_q1, lambda_k1, lambda_q2, lambda_k2, da_subln, hg_lb_fwd, hg_lb_bwd, hg_norm,
           w_in_odd, sg_norm, sg_w, sg_b, w_out_odd, w_ffn_in, w_ffn_out):
    b, s, d = x.shape
    t = b * s
    lb_f = _lower_bounds(hg_lb_fwd)
    lb_b = _lower_bounds(hg_lb_bwd)
    band = _band_call(rel_bias.astype(F32))
    n_attn = 3 * DA_WIDTH
    h = x.reshape(t, d)
    for l in range(DEPTH):
        gain = norm_mix[l].reshape(1, d)
        ffn = (norm_ffn[l].reshape(1, d), w_ffn_in[l].astype(BF16), w_ffn_out[l].astype(BF16),
               norm_final.reshape(1, d), l == DEPTH - 1)
        if l % 2 == 0:
            e = l // 2
            lambda_init = 0.8 - 0.6 * math.exp(-0.3 * l)
            w_in = w_in_even[e].astype(BF16)
            qkv, hg = _even_in_call(h, gain, w_in, n_attn)
            lamv = jnp.stack([lambda_q1[e], lambda_k1[e], lambda_q2[e], lambda_k2[e]]).astype(F32)
            o_a, o_f, o_b = _even_core_call(
                qkv.reshape(b, s, n_attn), hg.reshape(b, s, -1), band, rel_bias.astype(F32), lamv,
                da_subln[e].reshape(1, DA_V_DIM), lb_f[e], lb_b[e], lambda_init)
            h = _even_tail_call(o_a.reshape(t, DA_WIDTH), o_f.reshape(t, HG_WIDTH),
                                o_b.reshape(t, HG_WIDTH), hg, h,
                                hg_norm[e].reshape(1, HG_V), w_out_even[e].astype(BF16), *ffn)
        else:
            o = l // 2
            sg_b_full = jnp.repeat(sg_b[o].T.astype(F32), SG_GROUP_DIM, axis=1)
            h = _odd_layer_call(h, gain, w_in_odd[o].astype(BF16), sg_norm[o].reshape(1, d),
                                sg_w[o].astype(BF16), sg_b_full, w_out_odd[o].astype(BF16), *ffn)
    return h.reshape(b, s, d)
```
